```python
import math
import jax
import jax.numpy as jnp
from jax import lax
import numpy as np

D_MODEL = 1024
BATCH = 8
SEQ = 2048
DEPTH = 4
DEC_BATCH = 128
DEC_SEQ = 8
PAST_LEN = 16384
PAGE_SIZE = 128

N_MIXERS = 3
CONV_W = 4
EPS = 1e-6
H_A = 8
DK_A = 128
DV_A = 128
D_QK_A = H_A * DK_A
D_V_A = H_A * DV_A
D_CONV_A = 2 * D_QK_A + D_V_A
D_IN_A = D_CONV_A + D_V_A + 2 * H_A
CHUNK_A = 64
CHUNK_B = 128
D_B = 2 * D_MODEL
H_B = 8
DG_B = D_B // H_B
D_IN_B = 2 * D_B
D_INNER_C = 2 * D_MODEL
P_C = 64
H_C = D_INNER_C // P_C
G_C = 4
R_C = H_C // G_C
N_C = 128
D_XBC_C = D_INNER_C + 2 * G_C * N_C
D_IN_C = D_INNER_C + D_XBC_C + H_C
CHUNK_C = 128
D_FF = 2816
N_LAYERS_A = (DEPTH + 2) // 3
N_LAYERS_B = (DEPTH + 1) // 3
N_LAYERS_C = DEPTH // 3

kernel_name = "hybrid_gdn_chunkmlp_ssd_decoder_step"


def rms_norm(x, w):
    xf = x.astype(jnp.float32)
    y = xf * lax.rsqrt(jnp.mean(xf * xf, axis=-1, keepdims=True) + EPS)
    return (y * w.astype(jnp.float32)).astype(x.dtype)


def layer_norm(x, w, b):
    xf = x.astype(jnp.float32)
    mu = jnp.mean(xf, axis=-1, keepdims=True)
    var = jnp.mean(jnp.square(xf - mu), axis=-1, keepdims=True)
    return ((xf - mu) * lax.rsqrt(var + EPS) * w.astype(jnp.float32) + b.astype(jnp.float32)).astype(x.dtype)


def l2norm(x):
    xf = x.astype(jnp.float32)
    return xf * lax.rsqrt(jnp.sum(xf * xf, axis=-1, keepdims=True) + EPS)


def swiglu(x, w_in, w_out):
    g, u = jnp.split(x @ w_in, 2, axis=-1)
    return (jax.nn.silu(g) * u) @ w_out


def half_ffn(x, w_pre, w_post, w_in, w_out):
    return x + 0.5 * rms_norm(swiglu(rms_norm(x, w_pre), w_in, w_out), w_post)


def causal_dwconv(x, buf, w):
    L = x.shape[1]
    xp = jnp.concatenate([buf.astype(x.dtype), x], axis=1)
    y = sum(xp[:, j:j + L] * w[j] for j in range(CONV_W))
    return y, xp[:, L:]


def to_chunks(a, c):
    B, L = a.shape[:2]
    n = -(-L // c)
    a = jnp.pad(a, [(0, 0), (0, n * c - L)] + [(0, 0)] * (a.ndim - 2))
    return a.reshape((B, n, c) + a.shape[2:])


def gated_delta_chunked(q, k, v, g, beta, s0):
    B, L = q.shape[:2]
    c = min(CHUNK_A, L)
    f32 = jnp.float32
    q, k, v = [to_chunks(a.astype(f32), c).swapaxes(2, 3) for a in (q, k, v)]
    g, beta = [to_chunks(a.astype(f32), c).swapaxes(2, 3) for a in (g, beta)]
    G = jnp.cumsum(g, axis=-1)
    causal = jnp.tril(jnp.ones((c, c), bool))
    strict = jnp.tril(jnp.ones((c, c), bool), -1)
    gamma = jnp.exp(jnp.where(causal, G[..., :, None] - G[..., None, :], -jnp.inf))
    kb = k * beta[..., None]
    m = jnp.where(strict, jnp.einsum('bzhid,bzhjd->bzhij', kb, k) * gamma, 0.0)
    rhs = jnp.concatenate([v * beta[..., None], kb * jnp.exp(G)[..., None]], axis=-1)
    sol = lax.linalg.triangular_solve(jnp.eye(c, dtype=f32) + m, rhs, left_side=True, lower=True,
                                      unit_diagonal=True)
    u_ps, w_ps = sol[..., :DV_A], sol[..., DV_A:]
    attn = jnp.einsum('bzhid,bzhjd->bzhij', q, k) * gamma
    q_dec = q * jnp.exp(G)[..., None]
    k_dec = k * jnp.exp(G[..., -1:] - G)[..., None]
    g_tot = jnp.exp(G[..., -1])

    def step(S, inp):
        u_c, w_c, a_c, qd_c, kd_c, gt_c = inp
        v_new = u_c - jnp.einsum('bhid,bhde->bhie', w_c, S)
        o = jnp.einsum('bhid,bhde->bhie', qd_c, S) + jnp.einsum('bhij,bhje->bhie', a_c, v_new)
        S = S * gt_c[..., None, None] + jnp.einsum('bhjd,bhje->bhde', kd_c, v_new)
        return S, o

    xs = tuple(jnp.moveaxis(a, 1, 0) for a in (u_ps, w_ps, attn, q_dec, k_dec, g_tot))
    s_fin, o = lax.scan(step, s0.astype(f32), xs)
    o = jnp.moveaxis(o, 0, 1).swapaxes(2, 3)
    o = o.reshape(B, -1, H_A, DV_A)[:, :L]
    return o, s_fin


def ssd_chunked(x, dt, a, bm, cm, s0):
    B, L = x.shape[:2]
    c = min(CHUNK_C, L)
    f32 = jnp.float32
    x = to_chunks(x.astype(f32), c).reshape(B, -1, c, G_C, R_C, P_C)
    dt = to_chunks(dt.astype(f32), c).reshape(B, -1, c, G_C, R_C)
    bm = to_chunks(bm.astype(f32), c)
    cm = to_chunks(cm.astype(f32), c)
    G = jnp.cumsum(dt * a.astype(f32).reshape(G_C, R_C), axis=2)
    causal = jnp.tril(jnp.ones((c, c), bool))[:, :, None, None]
    seg = jnp.exp(jnp.where(causal, G[:, :, :, None] - G[:, :, None, :], -jnp.inf))
    xdt = x * dt[..., None]
    cb = jnp.einsum('bzigs,bzjgs->bzgij', cm, bm)
    y_diag = jnp.einsum('bzgij,bzijgr,bzjgrp->bzigrp', cb, seg, xdt)
    dec_end = jnp.exp(G[:, :, -1:] - G)
    states = jnp.einsum('bzjgs,bzjgr,bzjgrp->bzgrps', bm, dec_end, xdt)
    chunk_dec = jnp.exp(G[:, :, -1])

    def step(S, inp):
        st, gd = inp
        return S * gd[..., None, None] + st, S

    s_init = s0.astype(f32).reshape(B, G_C, R_C, P_C, N_C)
    s_fin, s_prev = lax.scan(step, s_init, (jnp.moveaxis(states, 1, 0), jnp.moveaxis(chunk_dec, 1, 0)))
    s_prev = jnp.moveaxis(s_prev, 0, 1)
    y_off = jnp.einsum('bzigs,bzigr,bzgrps->bzigrp', cm, jnp.exp(G), s_prev)
    y = (y_diag + y_off).reshape(B, -1, H_C, P_C)[:, :L]
    return y, s_fin.reshape(B, H_C, P_C, N_C)


def gdn_mixer(h, conv_buf, s0, w_in, conv_w, a_log, dt_bias, norm_w, w_out):
    B, L, _ = h.shape
    qkv, gate, a_in, b_in = jnp.split(h @ w_in, [D_CONV_A, D_CONV_A + D_V_A, D_CONV_A + D_V_A + H_A], axis=-1)
    qkv, new_buf = causal_dwconv(qkv, conv_buf, conv_w)
    qkv = jax.nn.silu(qkv)
    q, k, v = jnp.split(qkv, [D_QK_A, 2 * D_QK_A], axis=-1)
    q = l2norm(q.reshape(B, L, H_A, DK_A)) * (DK_A ** -0.5)
    k = l2norm(k.reshape(B, L, H_A, DK_A))
    v = v.reshape(B, L, H_A, DV_A)
    g = -jnp.exp(a_log.astype(jnp.float32)) * jax.nn.softplus(a_in.astype(jnp.float32) + dt_bias.astype(jnp.float32))
    beta = jax.nn.sigmoid(b_in.astype(jnp.float32))
    o, s_new = gated_delta_chunked(q, k, v, g, beta, s0)
    o = rms_norm(o.astype(h.dtype), norm_w) * jax.nn.silu(gate.reshape(B, L, H_A, DV_A))
    return o.reshape(B, L, D_V_A) @ w_out, (new_buf, s_new.astype(s0.dtype))


def chunk_mlp_mixer(h, w_in, b_in, ln_w, ln_b, w_s, b_s, w_out):
    B, L, _ = h.shape
    u, v = jnp.split(jax.nn.gelu(h @ w_in + b_in, approximate=False), 2, axis=-1)
    v = layer_norm(v, ln_w, ln_b)
    vc = to_chunks(v, CHUNK_B).reshape(B, -1, CHUNK_B, H_B, DG_B)
    ws = jnp.where(jnp.tril(jnp.ones((CHUNK_B, CHUNK_B), bool)), w_s, 0.0)
    mixed = jnp.einsum('hts,bzshd->bzthd', ws, vc) + b_s.T[None, None, :, :, None]
    mixed = mixed.reshape(B, -1, D_B)[:, :L]
    return (u * mixed) @ w_out, (v,)


def ssd_mixer(h, conv_buf, s0, w_in, conv_w, conv_b, dt_bias, a_log, d_skip, norm_w, w_out):
    B, L, _ = h.shape
    z, xbc, dt = jnp.split(h @ w_in, [D_INNER_C, D_INNER_C + D_XBC_C], axis=-1)
    xbc, new_buf = causal_dwconv(xbc, conv_buf, conv_w)
    xbc = jax.nn.silu(xbc + conv_b)
    x, bm, cm = jnp.split(xbc, [D_INNER_C, D_INNER_C + G_C * N_C], axis=-1)
    x = x.reshape(B, L, H_C, P_C)
    bm = bm.reshape(B, L, G_C, N_C)
    cm = cm.reshape(B, L, G_C, N_C)
    dt = jax.nn.softplus(dt.astype(jnp.float32) + dt_bias.astype(jnp.float32))
    a = -jnp.exp(a_log.astype(jnp.float32))
    y, s_new = ssd_chunked(x, dt, a, bm, cm, s0)
    y = y + x.astype(jnp.float32) * d_skip.astype(jnp.float32)[:, None]
    yg = (y.reshape(B, L, D_INNER_C).astype(h.dtype) * jax.nn.silu(z)).reshape(B, L, G_C, D_INNER_C // G_C)
    y = rms_norm(yg, norm_w.reshape(G_C, -1)).reshape(B, L, D_INNER_C)
    return y @ w_out, (new_buf, s_new.astype(s0.dtype))


def layer(i, x, conv_buf, rec_state, W):
    kind, j = i % N_MIXERS, i // N_MIXERS
    nw = W['norm_w'][i]
    x = half_ffn(x, nw[0], nw[1], W['ffn_w_in'][i, 0], W['ffn_w_out'][i, 0])
    h = rms_norm(x, nw[2])
    if kind == 0:
        m, new_state = gdn_mixer(h, conv_buf, rec_state, W['gdn_w_in'][j], W['gdn_conv_w'][j], W['gdn_a_log'][j],
                                 W['gdn_dt_bias'][j], W['gdn_norm_w'][j], W['gdn_w_out'][j])
    elif kind == 1:
        m, new_state = chunk_mlp_mixer(h, W['cmlp_w_in'][j], W['cmlp_b_in'][j], W['cmlp_ln_w'][j], W['cmlp_ln_b'][j],
                                       W['cmlp_w_s'][j], W['cmlp_b_s'][j], W['cmlp_w_out'][j])
    else:
        m, new_state = ssd_mixer(h, conv_buf, rec_state, W['ssd_w_in'][j], W['ssd_conv_w'][j], W['ssd_conv_b'][j],
                                 W['ssd_dt_bias'][j], W['ssd_a_log'][j], W['ssd_d'][j], W['ssd_norm_w'][j],
                                 W['ssd_w_out'][j])
    x = x + rms_norm(m, nw[3])
    x = half_ffn(x, nw[4], nw[5], W['ffn_w_in'][i, 1], W['ffn_w_out'][i, 1])
    return x, new_state


def _dt_bias(k, shape):
    dt = jnp.exp(jax.random.uniform(k, shape, minval=math.log(1e-3), maxval=math.log(1e-1)))
    return dt + jnp.log(-jnp.expm1(-dt))


def setup_inputs(seed: int = 0) -> dict:
    key = jax.random.key(seed)
    ks = iter(jax.random.split(key, 40))
    nrm = lambda shape, s: jax.random.normal(next(ks), shape, jnp.float32) * s
    gain = lambda shape: 1.0 + nrm(shape, 0.05)
    a_log = lambda shape: jnp.log(jax.random.uniform(next(ks), shape, minval=1.0, maxval=16.0))
    return {
        'x_prompt': nrm((BATCH, SEQ, D_MODEL), 1.0),
        'x_sample': nrm((DEC_BATCH, DEC_SEQ, D_MODEL), 1.0),
        'state_gdn': nrm((N_LAYERS_A, DEC_BATCH, H_A, DK_A, DV_A), 0.5),
        'state_gdn_conv': nrm((N_LAYERS_A, DEC_BATCH, CONV_W - 1, D_CONV_A), 1.0),
        'state_ssd': nrm((N_LAYERS_C, DEC_BATCH, H_C, P_C, N_C), 0.1),
        'state_ssd_conv': nrm((N_LAYERS_C, DEC_BATCH, CONV_W - 1, D_XBC_C), 1.0),
        'norm_w': gain((DEPTH, 6, D_MODEL)),
        'ffn_w_in': nrm((DEPTH, 2, D_MODEL, 2 * D_FF), D_MODEL ** -0.5),
        'ffn_w_out': nrm((DEPTH, 2, D_FF, D_MODEL), D_FF ** -0.5),
        'gdn_w_in': nrm((N_LAYERS_A, D_MODEL, D_IN_A), D_MODEL ** -0.5),
        'gdn_conv_w': nrm((N_LAYERS_A, CONV_W, D_CONV_A), CONV_W ** -0.5),
        'gdn_a_log': a_log((N_LAYERS_A, H_A)),
        'gdn_dt_bias': _dt_bias(next(ks), (N_LAYERS_A, H_A)),
        'gdn_norm_w': gain((N_LAYERS_A, DV_A)),
        'gdn_w_out': nrm((N_LAYERS_A, D_V_A, D_MODEL), D_V_A ** -0.5),
        'cmlp_w_in': nrm((N_LAYERS_B, D_MODEL, D_IN_B), D_MODEL ** -0.5),
        'cmlp_b_in': nrm((N_LAYERS_B, D_IN_B), 0.02),
        'cmlp_ln_w': gain((N_LAYERS_B, D_B)),
        'cmlp_ln_b': nrm((N_LAYERS_B, D_B), 0.02),
        'cmlp_w_s': nrm((N_LAYERS_B, H_B, CHUNK_B, CHUNK_B), CHUNK_B ** -0.5),
        'cmlp_b_s': 1.0 + nrm((N_LAYERS_B, H_B, CHUNK_B), 0.1),
        'cmlp_w_out': nrm((N_LAYERS_B, D_B, D_MODEL), D_B ** -0.5),
        'ssd_w_in': nrm((N_LAYERS_C, D_MODEL, D_IN_C), D_MODEL ** -0.5),
        'ssd_conv_w': nrm((N_LAYERS_C, CONV_W, D_XBC_C), CONV_W ** -0.5),
        'ssd_conv_b': nrm((N_LAYERS_C, D_XBC_C), 0.02),
        'ssd_dt_bias': _dt_bias(next(ks), (N_LAYERS_C, H_C)),
        'ssd_a_log': a_log((N_LAYERS_C, H_C)),
        'ssd_d': gain((N_LAYERS_C, H_C)),
        'ssd_norm_w': gain((N_LAYERS_C, D_INNER_C)),
        'ssd_w_out': nrm((N_LAYERS_C, D_INNER_C, D_MODEL), D_INNER_C ** -0.5),
    }


def reference(x_prompt, x_sample, state_gdn, state_gdn_conv, state_ssd, state_ssd_conv,
              norm_w, ffn_w_in, ffn_w_out,
              gdn_w_in, gdn_conv_w, gdn_a_log, gdn_dt_bias, gdn_norm_w, gdn_w_out,
              cmlp_w_in, cmlp_b_in, cmlp_ln_w, cmlp_ln_b, cmlp_w_s, cmlp_b_s, cmlp_w_out,
              ssd_w_in, ssd_conv_w, ssd_conv_b, ssd_dt_bias, ssd_a_log, ssd_d, ssd_norm_w, ssd_w_out):
    W = dict(norm_w=norm_w, ffn_w_in=ffn_w_in, ffn_w_out=ffn_w_out,
             gdn_w_in=gdn_w_in, gdn_conv_w=gdn_conv_w, gdn_a_log=gdn_a_log, gdn_dt_bias=gdn_dt_bias,
             gdn_norm_w=gdn_norm_w, gdn_w_out=gdn_w_out,
             cmlp_w_in=cmlp_w_in, cmlp_b_in=cmlp_b_in, cmlp_ln_w=cmlp_ln_w, cmlp_ln_b=cmlp_ln_b,
             cmlp_w_s=cmlp_w_s, cmlp_b_s=cmlp_b_s, cmlp_w_out=cmlp_w_out,
             ssd_w_in=ssd_w_in, ssd_conv_w=ssd_conv_w, ssd_conv_b=ssd_conv_b, ssd_dt_bias=ssd_dt_bias,
             ssd_a_log=ssd_a_log, ssd_d=ssd_d, ssd_norm_w=ssd_norm_w, ssd_w_out=ssd_w_out)
    xp, xs = x_prompt, x_sample
    nb, dt_p = xp.shape[0], xp.dtype
    gdn_p, gdn_conv_p, ssd_p, ssd_conv_p = [], [], [], []
    gdn_s, gdn_conv_s, ssd_s, ssd_conv_s, cmlp_s = [], [], [], [], []
    for i in range(DEPTH):
        kind, j = i % N_MIXERS, i // N_MIXERS
        if kind == 0:
            bp = jnp.zeros((nb, CONV_W - 1, D_CONV_A), dt_p)
            sp = jnp.zeros((nb, H_A, DK_A, DV_A), dt_p)
            bs, ss = state_gdn_conv[j], state_gdn[j]
        elif kind == 2:
            bp = jnp.zeros((nb, CONV_W - 1, D_XBC_C), dt_p)
            sp = jnp.zeros((nb, H_C, P_C, N_C), dt_p)
            bs, ss = state_ssd_conv[j], state_ssd[j]
        else:
            bp = sp = bs = ss = None
        xp, new_p = layer(i, xp, bp, sp, W)
        xs, new_s = layer(i, xs, bs, ss, W)
        if kind == 0:
            gdn_conv_p.append(new_p[0]); gdn_p.append(new_p[1])
            gdn_conv_s.append(new_s[0]); gdn_s.append(new_s[1])
        elif kind == 1:
            cmlp_s.append(new_s[0])
        else:
            ssd_conv_p.append(new_p[0]); ssd_p.append(new_p[1])
            ssd_conv_s.append(new_s[0]); ssd_s.append(new_s[1])
    return (xp, xs,
            jnp.stack(gdn_p), jnp.stack(gdn_conv_p), jnp.stack(ssd_p), jnp.stack(ssd_conv_p),
            jnp.stack(gdn_s), jnp.stack(gdn_conv_s), jnp.stack(ssd_s), jnp.stack(ssd_conv_s),
            jnp.stack(cmlp_s))
```

```python
import functools
import math

import jax
import jax.numpy as jnp
from jax import lax
from jax.experimental import pallas as pl
from jax.experimental.pallas import tpu as pltpu

F32 = jnp.float32
BF16 = jnp.bfloat16
HI = lax.Precision.HIGHEST

D_MODEL = 1024
D_FF = 2816
EPS = 1e-6
CONV_W = 4
H_A, DK_A, DV_A = 8, 128, 128
D_QK_A, D_V_A = H_A * DK_A, H_A * DV_A
D_CONV_A = 2 * D_QK_A + D_V_A
CHUNK_A = 64
CHUNK_B, D_B, H_B = 128, 2 * D_MODEL, 8
DG_B = D_B // H_B
D_INNER_C, P_C, G_C, N_C = 2 * D_MODEL, 64, 4, 128
H_C = D_INNER_C // P_C
R_C = H_C // G_C
D_XBC_C = D_INNER_C + 2 * G_C * N_C
CHUNK_C = 128

LANES = 128
ROW_TILE = 256
VMEM_LIMIT_BYTES = 56 * 1024 * 1024
CONV_PAD = 8

_NT = (((1,), (1,)), ((), ()))
_TN = (((0,), (0,)), ((), ()))


def _dot(a, b):
    return jnp.dot(a, b, preferred_element_type=F32)


def _dot_hi(a, b):
    return jnp.dot(a, b, precision=HI, preferred_element_type=F32)


def _dot_nt(a, b):
    return lax.dot_general(a, b, _NT, preferred_element_type=F32)


def _dot_tn(a, b):
    return lax.dot_general(a, b, _TN, preferred_element_type=F32)


def _silu(x):
    return x * jax.nn.sigmoid(x)


def _softplus(x):
    return jnp.maximum(x, 0.0) + jnp.log1p(jnp.exp(-jnp.abs(x)))


def _rms(x, w):
    return x * lax.rsqrt(jnp.mean(x * x, axis=-1, keepdims=True) + EPS) * w


def _half_ffn(x, w_pre, w_post, wi_ref, wo_ref):
    h = _rms(x, w_pre).astype(BF16)
    gu = _dot(h, wi_ref[...])
    a = (_silu(gu[:, :D_FF]) * gu[:, D_FF:]).astype(BF16)
    return x + 0.5 * _rms(_dot(a, wo_ref[...]), w_post)


def _pre_kernel(x_ref, nw_ref, wi_ref, wo_ref, wm_ref, ws_ref, x1_ref, pm_ref, ps_ref):
    nw = nw_ref[...]
    x1 = _half_ffn(x_ref[...], nw[0:1], nw[1:2], wi_ref, wo_ref)
    x1_ref[...] = x1
    h = _rms(x1, nw[2:3]).astype(BF16)
    pm_ref[...] = _dot(h, wm_ref[...])
    ps_ref[...] = _dot(h, ws_ref[...])


def _pre_cmlp_kernel(x_ref, nw_ref, wi_ref, wo_ref, wm_ref, b_ref, lnw_ref, lnb_ref, x1_ref, u_ref, v_ref):
    nw = nw_ref[...]
    x1 = _half_ffn(x_ref[...], nw[0:1], nw[1:2], wi_ref, wo_ref)
    x1_ref[...] = x1
    h = _rms(x1, nw[2:3]).astype(BF16)
    p = _dot(h, wm_ref[...]) + b_ref[...]
    ge = 0.5 * p * (1.0 + lax.erf(p * (1.0 / math.sqrt(2.0))))
    u_ref[...] = ge[:, :D_B]
    v = ge[:, D_B:]
    mu = jnp.mean(v, axis=-1, keepdims=True)
    vc = v - mu
    var = jnp.mean(vc * vc, axis=-1, keepdims=True)
    v_ref[...] = vc * lax.rsqrt(var + EPS) * lnw_ref[...] + lnb_ref[...]


def _post_kernel(x1_ref, o_ref, wom_ref, nw_ref, wi_ref, wo_ref, x3_ref):
    nw = nw_ref[...]
    m = _dot(o_ref[...].astype(BF16), wom_ref[...])
    x2 = x1_ref[...] + _rms(m, nw[3:4])
    x3_ref[...] = _half_ffn(x2, nw[4:5], nw[5:6], wi_ref, wo_ref)


def _post_cmlp_kernel(x1_ref, u_ref, v_ref, wmix_ref, bmix_ref, wom_ref, nw_ref, wi_ref, wo_ref, x3_ref, a_ref):
    nw = nw_ref[...]
    row = lax.broadcasted_iota(jnp.int32, (CHUNK_B, CHUNK_B), 0)
    col = lax.broadcasted_iota(jnp.int32, (CHUNK_B, CHUNK_B), 1)
    bmix = bmix_ref[...]
    for h in range(H_B):
        w = jnp.where(row >= col, wmix_ref[h], 0.0).astype(BF16)
        cs = slice(h * DG_B, (h + 1) * DG_B)
        for r in range(ROW_TILE // CHUNK_B):
            rs = slice(r * CHUNK_B, (r + 1) * CHUNK_B)
            mixed = _dot(w, v_ref[rs, cs].astype(BF16)) + bmix[:, h:h + 1]
            a_ref[rs, cs] = (u_ref[rs, cs] * mixed).astype(BF16)
    m = _dot(a_ref[...], wom_ref[...])
    x2 = x1_ref[...] + _rms(m, nw[3:4])
    x3_ref[...] = _half_ffn(x2, nw[4:5], nw[5:6], wi_ref, wo_ref)


def _resident(shape):
    nd = len(shape)
    return pl.BlockSpec(shape, lambda i: (0,) * nd, pipeline_mode=pl.Buffered(1))


def _rows(width):
    return pl.BlockSpec((ROW_TILE, width), lambda i: (i, 0))


def _row_call(body, n_rows, in_arrays, in_specs, out_widths, scratch_shapes=()):
    return pl.pallas_call(
        body,
        grid=(n_rows // ROW_TILE,),
        in_specs=in_specs,
        out_specs=[_rows(w) for w in out_widths],
        out_shape=[jax.ShapeDtypeStruct((n_rows, w), F32) for w in out_widths],
        scratch_shapes=list(scratch_shapes),
        compiler_params=pltpu.CompilerParams(dimension_semantics=("arbitrary",),
                                             vmem_limit_bytes=VMEM_LIMIT_BYTES),
    )(*in_arrays)


def _pre_call(x, nw, wi, wo, wm, ws):
    t = x.shape[0]
    return _row_call(
        _pre_kernel, t, (x, nw, wi, wo, wm, ws),
        [_rows(D_MODEL), _resident(nw.shape), _resident(wi.shape), _resident(wo.shape),
         _resident(wm.shape), _resident(ws.shape)],
        (D_MODEL, wm.shape[1], ws.shape[1]))


def _pre_cmlp_call(x, nw, wi, wo, wm, b, lnw, lnb):
    t = x.shape[0]
    return _row_call(
        _pre_cmlp_kernel, t, (x, nw, wi, wo, wm, b, lnw, lnb),
        [_rows(D_MODEL), _resident(nw.shape), _resident(wi.shape), _resident(wo.shape),
         _resident(wm.shape), _resident(b.shape), _resident(lnw.shape), _resident(lnb.shape)],
        (D_MODEL, D_B, D_B))


def _post_call(x1, o, wom, nw, wi, wo):
    t = x1.shape[0]
    return _row_call(
        _post_kernel, t, (x1, o, wom, nw, wi, wo),
        [_rows(D_MODEL), _rows(o.shape[1]), _resident(wom.shape), _resident(nw.shape),
         _resident(wi.shape), _resident(wo.shape)],
        (D_MODEL,))[0]


def _post_cmlp_call(x1, u, v, wmix, bmix, wom, nw, wi, wo, n_prompt_rows):
    t = x1.shape[0]
    n_prompt_tiles = n_prompt_rows // ROW_TILE
    return _row_call(
        _post_cmlp_kernel, t, (x1, u, v, wmix, bmix, wom, nw, wi, wo),
        [_rows(D_MODEL), _rows(D_B), _rows(D_B),
         pl.BlockSpec((None, H_B, CHUNK_B, CHUNK_B), lambda i: (i // n_prompt_tiles, 0, 0, 0)),
         pl.BlockSpec((None, CHUNK_B, LANES), lambda i: (i // n_prompt_tiles, 0, 0)),
         _resident(wom.shape), _resident(nw.shape), _resident(wi.shape), _resident(wo.shape)],
        (D_MODEL,),
        scratch_shapes=[pltpu.VMEM((ROW_TILE, D_B), BF16)])[0]


def _causal_conv(x, xp_ref, cw, c):
    xp_ref[CONV_PAD:CONV_PAD + c, :] = x
    y = cw[3:4] * x
    for j in range(CONV_W - 1):
        y = y + cw[j:j + 1] * xp_ref[CONV_PAD - 3 + j:CONV_PAD - 3 + j + c, :]
    tail = xp_ref[CONV_PAD + c - 3:CONV_PAD + c, :]
    xp_ref[CONV_PAD - 3:CONV_PAD, :] = tail
    return y, tail


def _cumsum_rows(g, c):
    row = lax.broadcasted_iota(jnp.int32, (c, c), 0)
    col = lax.broadcasted_iota(jnp.int32, (c, c), 1)
    tril = (row >= col).astype(F32)
    big = _dot_hi(tril, g)
    big_t = lax.dot_general(g, tril, (((0,), (1,)), ((), ())), precision=HI, preferred_element_type=F32)
    return big, big_t, row, col


def _gdn_kernel(pm_ref, ps_ref, cb_ref, s0_ref, cw_ref, alog_ref, dtb_ref, nw_ref,
                og_ref, ncb_ref, s_ref, xp_ref, *, c):
    @pl.when(pl.program_id(1) == 0)
    def _():
        xp_ref[CONV_PAD - 3:CONV_PAD, :] = cb_ref[...]
        s_ref[...] = s0_ref[...]

    y, tail = _causal_conv(pm_ref[:, :D_CONV_A], xp_ref, cw_ref[...], c)
    ncb_ref[...] = tail
    y = _silu(y)

    sm = ps_ref[...]
    g = -jnp.exp(alog_ref[...]) * _softplus(sm + dtb_ref[...])
    beta_all = jax.nn.sigmoid(sm)
    big, big_t, row, col = _cumsum_rows(g, c)
    g_last = big[c - 1:c, :]
    eye = (row == col).astype(F32)
    nw = nw_ref[...]

    for h in range(H_A):
        hs = slice(h * DK_A, (h + 1) * DK_A)
        q = y[:, hs]
        k = y[:, D_QK_A + h * DK_A:D_QK_A + (h + 1) * DK_A]
        v = y[:, 2 * D_QK_A + h * DV_A:2 * D_QK_A + (h + 1) * DV_A]
        q = q * lax.rsqrt(jnp.sum(q * q, axis=-1, keepdims=True) + EPS) * (DK_A ** -0.5)
        k = k * lax.rsqrt(jnp.sum(k * k, axis=-1, keepdims=True) + EPS)
        gc = big[:, h:h + 1]
        gr = big_t[h:h + 1, :]
        gl = g_last[:, h:h + 1]
        beta = beta_all[:, H_A + h:H_A + h + 1]
        gamma = jnp.exp(jnp.where(row >= col, gc - gr, -jnp.inf))
        kb = k * beta
        k16 = k.astype(BF16)
        m = jnp.where(row > col, _dot_nt(kb.astype(BF16), k16) * gamma, 0.0)
        p = -m
        t = eye + p
        for _ in range(int(math.log2(c)) - 1):
            p = _dot_hi(p, p)
            t = t + _dot_hi(t, p)
        eg = jnp.exp(gc)
        sol = _dot_hi(t, jnp.concatenate([v * beta, kb * eg], axis=1))
        u, w = sol[:, :DV_A], sol[:, DV_A:]
        attn = _dot_nt(q.astype(BF16), k16) * gamma
        s = s_ref[h]
        s16 = s.astype(BF16)
        v_new = u - _dot(w.astype(BF16), s16)
        vn16 = v_new.astype(BF16)
        o = _dot((q * eg).astype(BF16), s16) + _dot(attn.astype(BF16), vn16)
        k_dec = k * jnp.exp(gl - gc)
        s_ref[h] = s * jnp.exp(gl) + _dot_tn(k_dec.astype(BF16), vn16)
        gate = pm_ref[:, D_CONV_A + h * DV_A:D_CONV_A + (h + 1) * DV_A]
        og_ref[:, h * DV_A:(h + 1) * DV_A] = _rms(o, nw) * _silu(gate)


def _ssd_kernel(pm_ref, ps_ref, cb_ref, s0_ref, cw_ref, cbias_ref, dtb_ref, alog_ref, dskip_ref, nw_ref,
                og_ref, ncb_ref, s_ref, xp_ref, y_ref, *, c):
    @pl.when(pl.program_id(1) == 0)
    def _():
        xp_ref[CONV_PAD - 3:CONV_PAD, :] = cb_ref[...]
        s_ref[...] = s0_ref[...]

    y, tail = _causal_conv(pm_ref[:, D_INNER_C:D_INNER_C + D_XBC_C], xp_ref, cw_ref[...], c)
    ncb_ref[...] = tail
    y = _silu(y + cbias_ref[...])
    x = y[:, :D_INNER_C]

    dt = _softplus(ps_ref[...] + dtb_ref[...])
    da = dt * (-jnp.exp(alog_ref[...]))
    big, big_t, row, col = _cumsum_rows(da, c)
    g_last = big[c - 1:c, :]
    e_row = lax.broadcasted_iota(jnp.int32, (LANES, D_INNER_C), 0)
    e_col = lax.broadcasted_iota(jnp.int32, (LANES, D_INNER_C), 1)
    expand = (e_col // P_C == e_row).astype(F32)
    dt_e = _dot_hi(dt, expand)
    eg_e = _dot_hi(jnp.exp(big), expand)
    dec_e = _dot_hi(jnp.exp(g_last - big), expand)
    xdt = x * dt_e
    xdt16 = xdt.astype(BF16)
    xdec16 = (xdt * dec_e).astype(BF16)
    causal = row >= col

    for g in range(G_C):
        bg = y[:, D_INNER_C + g * N_C:D_INNER_C + (g + 1) * N_C].astype(BF16)
        cg = y[:, D_INNER_C + (G_C + g) * N_C:D_INNER_C + (G_C + g + 1) * N_C].astype(BF16)
        gs = slice(g * R_C * P_C, (g + 1) * R_C * P_C)
        cbm = _dot_nt(cg, bg)
        s_g = s_ref[g * R_C:(g + 1) * R_C].reshape(R_C * P_C, N_C)
        y_off = _dot_nt(cg, s_g.astype(BF16)) * eg_e[:, gs]
        st = _dot_tn(xdec16[:, gs], bg)
        for r in range(R_C):
            h = g * R_C + r
            seg = jnp.exp(jnp.where(causal, big[:, h:h + 1] - big_t[h:h + 1, :], -jnp.inf))
            yd = _dot((cbm * seg).astype(BF16), xdt16[:, h * P_C:(h + 1) * P_C])
            y_ref[:, h * P_C:(h + 1) * P_C] = yd + y_off[:, r * P_C:(r + 1) * P_C]
            s_ref[h] = s_ref[h] * jnp.exp(g_last[:, h:h + 1]) + st[r * P_C:(r + 1) * P_C, :]

    zg = pm_ref[:, :D_INNER_C]
    yg = (y_ref[...] + x * dskip_ref[...]) * _silu(zg)
    nw = nw_ref[...]
    width = D_INNER_C // G_C
    for g in range(G_C):
        gs = slice(g * width, (g + 1) * width)
        og_ref[:, gs] = _rms(yg[:, gs], nw[:, gs])


def _seq_call(body, pm, ps, conv_buf, s0, params, *, n_seq, seq_len, c, row_off, d_out, d_conv, extra_scratch=()):
    nz = seq_len // c
    blk_off = row_off // c
    state_block = (None,) + s0.shape[1:]
    n_state = len(s0.shape) - 1

    def const(a):
        return pl.BlockSpec(a.shape, lambda b, z: (0,) * a.ndim)

    kernel = functools.partial(body, c=c)
    return pl.pallas_call(
        kernel,
        grid=(n_seq, nz),
        in_specs=[pl.BlockSpec((c, pm.shape[1]), lambda b, z: (blk_off + b * nz + z, 0)),
                  pl.BlockSpec((c, ps.shape[1]), lambda b, z: (blk_off + b * nz + z, 0)),
                  pl.BlockSpec((None, CONV_W - 1, d_conv), lambda b, z: (b, 0, 0)),
                  pl.BlockSpec(state_block, lambda b, z: (b,) + (0,) * n_state)]
                 + [const(a) for a in params],
        out_specs=[pl.BlockSpec((c, d_out), lambda b, z: (b * nz + z, 0)),
                   pl.BlockSpec((None, CONV_W - 1, d_conv), lambda b, z: (b, 0, 0)),
                   pl.BlockSpec(state_block, lambda b, z: (b,) + (0,) * n_state)],
        out_shape=[jax.ShapeDtypeStruct((n_seq * seq_len, d_out), F32),
                   jax.ShapeDtypeStruct((n_seq, CONV_W - 1, d_conv), F32),
                   jax.ShapeDtypeStruct(s0.shape, F32)],
        scratch_shapes=[pltpu.VMEM((CONV_PAD + c, d_conv), F32)] + list(extra_scratch(c)) if extra_scratch else
                       [pltpu.VMEM((CONV_PAD + c, d_conv), F32)],
        compiler_params=pltpu.CompilerParams(dimension_semantics=("arbitrary", "arbitrary"),
                                             vmem_limit_bytes=VMEM_LIMIT_BYTES),
    )(pm, ps, conv_buf, s0, *params)


def _pad_lanes(a):
    a = a.reshape(1, -1).astype(F32)
    return jnp.pad(a, ((0, 0), (0, LANES - a.shape[1])))


def _split_in_proj(w_in, n_main):
    wm = w_in[:, :n_main].astype(BF16)
    ws = jnp.pad(w_in[:, n_main:], ((0, 0), (0, LANES - (w_in.shape[1] - n_main)))).astype(BF16)
    return wm, ws


def kernel(x_prompt, x_sample, state_gdn, state_gdn_conv, state_ssd, state_ssd_conv, norm_w, ffn_w_in, ffn_w_out, gdn_w_in, gdn_conv_w, gdn_a_log, gdn_dt_bias, gdn_norm_w, gdn_w_out, cmlp_w_in, cmlp_b_in, cmlp_ln_w, cmlp_ln_b, cmlp_w_s, cmlp_b_s, cmlp_w_out, ssd_w_in, ssd_conv_w, ssd_conv_b, ssd_dt_bias, ssd_a_log, ssd_d, ssd_norm_w, ssd_w_out):
    nb, seq, _ = x_prompt.shape
    ndb, dseq, _ = x_sample.shape
    depth = norm_w.shape[0]
    n_p, n_s = nb * seq, ndb * dseq
    x = jnp.concatenate([x_prompt.reshape(n_p, D_MODEL), x_sample.reshape(n_s, D_MODEL)], axis=0)

    gdn_p, gdn_conv_p, ssd_p, ssd_conv_p = [], [], [], []
    gdn_s, gdn_conv_s, ssd_s, ssd_conv_s, cmlp_s = [], [], [], [], []
    for i in range(depth):
        kind, j = i % 3, i // 3
        nw = jnp.pad(norm_w[i], ((0, 2), (0, 0)))
        wi1, wo1 = ffn_w_in[i, 0].astype(BF16), ffn_w_out[i, 0].astype(BF16)
        wi2, wo2 = ffn_w_in[i, 1].astype(BF16), ffn_w_out[i, 1].astype(BF16)
        if kind == 0:
            wm, ws = _split_in_proj(gdn_w_in[j], D_CONV_A + D_V_A)
            x1, pm, ps = _pre_call(x, nw, wi1, wo1, wm, ws)
            params = (gdn_conv_w[j], _pad_lanes(gdn_a_log[j]), _pad_lanes(gdn_dt_bias[j]),
                      gdn_norm_w[j].reshape(1, DV_A))
            kw = dict(d_out=D_V_A, d_conv=D_CONV_A)
            o_p, cb_p, st_p = _seq_call(
                _gdn_kernel, pm, ps, jnp.zeros((nb, CONV_W - 1, D_CONV_A), F32),
                jnp.zeros((nb, H_A, DK_A, DV_A), F32), params,
                n_seq=nb, seq_len=seq, c=min(CHUNK_A, seq), row_off=0, **kw)
            o_s, cb_s, st_s = _seq_call(
                _gdn_kernel, pm, ps, state_gdn_conv[j], state_gdn[j], params,
                n_seq=ndb, seq_len=dseq, c=min(CHUNK_A, dseq), row_off=n_p, **kw)
            gdn_p.append(st_p); gdn_conv_p.append(cb_p); gdn_s.append(st_s); gdn_conv_s.append(cb_s)
            x = _post_call(x1, jnp.concatenate([o_p, o_s], axis=0), gdn_w_out[j].astype(BF16), nw, wi2, wo2)
        elif kind == 1:
            x1, u, v = _pre_cmlp_call(x, nw, wi1, wo1, cmlp_w_in[j].astype(BF16), cmlp_b_in[j].reshape(1, -1),
                                      cmlp_ln_w[j].reshape(1, -1), cmlp_ln_b[j].reshape(1, -1))
            cmlp_s.append(v[n_p:].reshape(ndb, dseq, D_B))
            reps = CHUNK_B // dseq
            w_s = cmlp_w_s[j]
            w_blk = jax.vmap(lambda m: jnp.kron(jnp.eye(reps, dtype=F32), m))(w_s[:, :dseq, :dseq])
            wmix = jnp.stack([w_s, w_blk])
            b_t = cmlp_b_s[j].T
            bmix = jnp.stack([b_t, jnp.tile(b_t[:dseq], (reps, 1))])
            bmix = jnp.pad(bmix, ((0, 0), (0, 0), (0, LANES - H_B)))
            x = _post_cmlp_call(x1, u, v, wmix, bmix, cmlp_w_out[j].astype(BF16), nw, wi2, wo2, n_p)
        else:
            wm, ws = _split_in_proj(ssd_w_in[j], D_INNER_C + D_XBC_C)
            x1, pm, ps = _pre_call(x, nw, wi1, wo1, wm, ws)
            params = (ssd_conv_w[j], ssd_conv_b[j].reshape(1, -1), _pad_lanes(ssd_dt_bias[j]),
                      _pad_lanes(ssd_a_log[j]), jnp.repeat(ssd_d[j], P_C).reshape(1, -1),
                      ssd_norm_w[j].reshape(1, -1))
            kw = dict(d_out=D_INNER_C, d_conv=D_XBC_C,
                      extra_scratch=lambda c: [pltpu.VMEM((c, D_INNER_C), F32)])
            o_p, cb_p, st_p = _seq_call(
                _ssd_kernel, pm, ps, jnp.zeros((nb, CONV_W - 1, D_XBC_C), F32),
                jnp.zeros((nb, H_C, P_C, N_C), F32), params,
                n_seq=nb, seq_len=seq, c=min(CHUNK_C, seq), row_off=0, **kw)
            o_s, cb_s, st_s = _seq_call(
                _ssd_kernel, pm, ps, state_ssd_conv[j], state_ssd[j], params,
                n_seq=ndb, seq_len=dseq, c=min(CHUNK_C, dseq), row_off=n_p, **kw)
            ssd_p.append(st_p); ssd_conv_p.append(cb_p); ssd_s.append(st_s); ssd_conv_s.append(cb_s)
            x = _post_call(x1, jnp.concatenate([o_p, o_s], axis=0), ssd_w_out[j].astype(BF16), nw, wi2, wo2)

    return (x[:n_p].reshape(nb, seq, D_MODEL), x[n_p:].reshape(ndb, dseq, D_MODEL),
            jnp.stack(gdn_p), jnp.stack(gdn_conv_p), jnp.stack(ssd_p), jnp.stack(ssd_conv_p),
            jnp.stack(gdn_s), jnp.stack(gdn_conv_s), jnp.stack(ssd_s), jnp.stack(ssd_conv_s),
            jnp.stack(cmlp_s))
```

```python
import functools
import math

import jax
import jax.numpy as jnp
from jax import lax
from jax.experimental import pallas as pl
from jax.experimental.pallas import tpu as pltpu

F32 = jnp.float32
BF16 = jnp.bfloat16
HI = lax.Precision.HIGHEST

D_MODEL = 1024
D_FF = 2816
EPS = 1e-6
CONV_W = 4
H_A, DK_A, DV_A = 8, 128, 128
D_QK_A, D_V_A = H_A * DK_A, H_A * DV_A
D_CONV_A = 2 * D_QK_A + D_V_A
CHUNK_A = 64
CHUNK_B, D_B, H_B = 128, 2 * D_MODEL, 8
DG_B = D_B // H_B
D_INNER_C, P_C, G_C, N_C = 2 * D_MODEL, 64, 4, 128
H_C = D_INNER_C // P_C
R_C = H_C // G_C
D_XBC_C = D_INNER_C + 2 * G_C * N_C
CHUNK_C = 128

LANES = 128
ROW_TILE = 256
VMEM_LIMIT_BYTES = 60 * 1024 * 1024
CONV_PAD = 8
PROJ_CHUNK = 1024

_NT = (((1,), (1,)), ((), ()))
_TN = (((0,), (0,)), ((), ()))


def _dot(a, b):
    return jnp.dot(a, b, preferred_element_type=F32)


def _dot_hi(a, b):
    return jnp.dot(a, b, precision=HI, preferred_element_type=F32)


def _dot_nt(a, b):
    return lax.dot_general(a, b, _NT, preferred_element_type=F32)


def _dot_tn(a, b):
    return lax.dot_general(a, b, _TN, preferred_element_type=F32)


def _silu(x):
    return x * jax.nn.sigmoid(x)


def _softplus(x):
    return jnp.maximum(x, 0.0) + jnp.log1p(jnp.exp(-jnp.abs(x)))


def _rms(x, w):
    return x * lax.rsqrt(jnp.mean(x * x, axis=-1, keepdims=True) + EPS) * w


def _half_ffn(x, w_pre, w_post, wi_ref, wo_ref):
    h = _rms(x, w_pre).astype(BF16)
    gu = _dot(h, wi_ref[...])
    a = (_silu(gu[:, :D_FF]) * gu[:, D_FF:]).astype(BF16)
    return x + 0.5 * _rms(_dot(a, wo_ref[...]), w_post)


def _dot_into(out_ref, h, w_ref):
    width = w_ref.shape[1]
    for c0 in range(0, width, PROJ_CHUNK):
        c1 = min(c0 + PROJ_CHUNK, width)
        out_ref[:, c0:c1] = _dot(h, w_ref[:, c0:c1])


def _read_rows(refs, is_p, dtype=None):
    vals = [r[...] if dtype is None else r[...].astype(dtype) for r in refs]
    return vals[0] if len(vals) == 1 else jnp.where(is_p, vals[0], vals[1])


def _write_rows(refs, is_p, val):
    if len(refs) == 1:
        refs[0][...] = val
        return

    @pl.when(is_p)
    def _():
        refs[0][...] = val

    @pl.when(jnp.logical_not(is_p))
    def _():
        refs[1][...] = val


def _pre_kernel(*refs, npt, n_x):
    x_refs = refs[:n_x]
    nw_ref, wi_ref, wo_ref, wm_ref, ws_ref, x1_ref, pm_ref, ps_ref = refs[n_x:]
    is_p = pl.program_id(0) < npt
    nw = nw_ref[...]
    x1 = _half_ffn(_read_rows(x_refs, is_p), nw[0:1], nw[1:2], wi_ref, wo_ref)
    x1_ref[...] = x1
    h = _rms(x1, nw[2:3]).astype(BF16)
    _dot_into(pm_ref, h, wm_ref)
    ps_ref[...] = _dot(h, ws_ref[...])


def _pre_cmlp_kernel(*refs, npt, n_x):
    x_refs = refs[:n_x]
    nw_ref, wi_ref, wo_ref, wm_ref, b_ref, lnw_ref, lnb_ref, x1_ref, u_ref, v_ref = refs[n_x:]
    is_p = pl.program_id(0) < npt
    nw = nw_ref[...]
    x1 = _half_ffn(_read_rows(x_refs, is_p), nw[0:1], nw[1:2], wi_ref, wo_ref)
    x1_ref[...] = x1
    h = _rms(x1, nw[2:3]).astype(BF16)
    p = _dot(h, wm_ref[...]) + b_ref[...]
    ge = 0.5 * p * (1.0 + lax.erf(p * (1.0 / math.sqrt(2.0))))
    u_ref[...] = ge[:, :D_B]
    v = ge[:, D_B:]
    mu = jnp.mean(v, axis=-1, keepdims=True)
    vc = v - mu
    var = jnp.mean(vc * vc, axis=-1, keepdims=True)
    v_ref[...] = vc * lax.rsqrt(var + EPS) * lnw_ref[...] + lnb_ref[...]


def _post_kernel(x1_ref, op_ref, os_ref, wom_ref, nw_ref, wi_ref, wo_ref, *x3_refs, npt):
    is_p = pl.program_id(0) < npt
    nw = nw_ref[...]
    m = _dot(_read_rows((op_ref, os_ref), is_p, BF16), wom_ref[...])
    x2 = x1_ref[...] + _rms(m, nw[3:4])
    _write_rows(x3_refs, is_p, _half_ffn(x2, nw[4:5], nw[5:6], wi_ref, wo_ref))


def _post_cmlp_kernel(x1_ref, u_ref, v_ref, wmix_ref, bmix_ref, wom_ref, nw_ref, wi_ref, wo_ref, *x3_and_scratch,
                      npt):
    x3_refs, a_ref = x3_and_scratch[:-1], x3_and_scratch[-1]
    is_p = pl.program_id(0) < npt
    nw = nw_ref[...]
    row = lax.broadcasted_iota(jnp.int32, (CHUNK_B, CHUNK_B), 0)
    col = lax.broadcasted_iota(jnp.int32, (CHUNK_B, CHUNK_B), 1)
    bmix = bmix_ref[...]
    for h in range(H_B):
        w = jnp.where(row >= col, wmix_ref[h], 0.0).astype(BF16)
        cs = slice(h * DG_B, (h + 1) * DG_B)
        for r in range(ROW_TILE // CHUNK_B):
            rs = slice(r * CHUNK_B, (r + 1) * CHUNK_B)
            mixed = _dot(w, v_ref[rs, cs].astype(BF16)) + bmix[:, h:h + 1]
            a_ref[rs, cs] = (u_ref[rs, cs] * mixed).astype(BF16)
    m = _dot(a_ref[...], wom_ref[...])
    x2 = x1_ref[...] + _rms(m, nw[3:4])
    _write_rows(x3_refs, is_p, _half_ffn(x2, nw[4:5], nw[5:6], wi_ref, wo_ref))


def _resident(a):
    nd = a.ndim
    return pl.BlockSpec(a.shape, lambda i: (0,) * nd, pipeline_mode=pl.Buffered(1))


def _rows(width):
    return pl.BlockSpec((ROW_TILE, width), lambda i: (i, 0))


def _pair_specs(width, npt):
    return [pl.BlockSpec((ROW_TILE, width), lambda i: (jnp.minimum(i, npt - 1), 0)),
            pl.BlockSpec((ROW_TILE, width), lambda i: (jnp.maximum(i - npt, 0), 0))]


def _x_specs(x, npt):
    return _pair_specs(D_MODEL, npt) if isinstance(x, tuple) else [_rows(D_MODEL)]


def _x_out(n_p, n_s, npt, pair):
    if pair:
        return _pair_specs(D_MODEL, npt), [jax.ShapeDtypeStruct((n_p, D_MODEL), F32),
                                           jax.ShapeDtypeStruct((n_s, D_MODEL), F32)]
    return [_rows(D_MODEL)], [jax.ShapeDtypeStruct((n_p + n_s, D_MODEL), F32)]


def _as_list(x):
    return list(x) if isinstance(x, tuple) else [x]


def _row_call(body, n_p, n_s, in_arrays, in_specs, out_specs, out_shape, scratch_shapes=()):
    assert n_p % ROW_TILE == 0 and n_s % ROW_TILE == 0
    return pl.pallas_call(
        body,
        grid=((n_p + n_s) // ROW_TILE,),
        in_specs=in_specs,
        out_specs=out_specs,
        out_shape=out_shape,
        scratch_shapes=list(scratch_shapes),
        compiler_params=pltpu.CompilerParams(dimension_semantics=("arbitrary",),
                                             vmem_limit_bytes=VMEM_LIMIT_BYTES),
    )(*in_arrays)


def _pre_call(x, n_p, n_s, nw, wi, wo, wm, ws):
    npt = n_p // ROW_TILE
    xs = _as_list(x)
    widths = (D_MODEL, wm.shape[1], ws.shape[1])
    return _row_call(
        functools.partial(_pre_kernel, npt=npt, n_x=len(xs)), n_p, n_s, (*xs, nw, wi, wo, wm, ws),
        _x_specs(x, npt) + [_resident(a) for a in (nw, wi, wo, wm, ws)],
        [_rows(w) for w in widths], [jax.ShapeDtypeStruct((n_p + n_s, w), F32) for w in widths])


def _pre_cmlp_call(x, n_p, n_s, nw, wi, wo, wm, b, lnw, lnb):
    npt = n_p // ROW_TILE
    xs = _as_list(x)
    widths = (D_MODEL, D_B, D_B)
    return _row_call(
        functools.partial(_pre_cmlp_kernel, npt=npt, n_x=len(xs)), n_p, n_s, (*xs, nw, wi, wo, wm, b, lnw, lnb),
        _x_specs(x, npt) + [_resident(a) for a in (nw, wi, wo, wm, b, lnw, lnb)],
        [_rows(w) for w in widths], [jax.ShapeDtypeStruct((n_p + n_s, w), F32) for w in widths])


def _post_call(x1, o, n_p, n_s, wom, nw, wi, wo, *, pair_out):
    npt = n_p // ROW_TILE
    out_specs, out_shape = _x_out(n_p, n_s, npt, pair_out)
    out = _row_call(
        functools.partial(_post_kernel, npt=npt), n_p, n_s, (x1, *o, wom, nw, wi, wo),
        [_rows(D_MODEL)] + _pair_specs(o[0].shape[1], npt) + [_resident(a) for a in (wom, nw, wi, wo)],
        out_specs, out_shape)
    return tuple(out) if pair_out else out[0]


def _post_cmlp_call(x1, u, v, n_p, n_s, wmix, bmix, wom, nw, wi, wo, *, pair_out):
    npt = n_p // ROW_TILE
    out_specs, out_shape = _x_out(n_p, n_s, npt, pair_out)
    out = _row_call(
        functools.partial(_post_cmlp_kernel, npt=npt), n_p, n_s, (x1, u, v, wmix, bmix, wom, nw, wi, wo),
        [_rows(D_MODEL), _rows(D_B), _rows(D_B),
         pl.BlockSpec((None, H_B, CHUNK_B, CHUNK_B), lambda i: (i // npt, 0, 0, 0)),
         pl.BlockSpec((None, CHUNK_B, LANES), lambda i: (i // npt, 0, 0))]
        + [_resident(a) for a in (wom, nw, wi, wo)],
        out_specs, out_shape, scratch_shapes=[pltpu.VMEM((ROW_TILE, D_B), BF16)])
    return tuple(out) if pair_out else out[0]


def _causal_conv(x, xp_ref, cw, c):
    xp_ref[CONV_PAD:CONV_PAD + c, :] = x
    y = cw[3:4] * x
    for j in range(CONV_W - 1):
        y = y + cw[j:j + 1] * xp_ref[CONV_PAD - 3 + j:CONV_PAD - 3 + j + c, :]
    tail = xp_ref[CONV_PAD + c - 3:CONV_PAD + c, :]
    xp_ref[CONV_PAD - 3:CONV_PAD, :] = tail
    return y, tail


def _seq_masks(rows, c):
    row = lax.broadcasted_iota(jnp.int32, (rows, rows), 0)
    col = lax.broadcasted_iota(jnp.int32, (rows, rows), 1)
    same = _same_block(row, col, c) if c < rows else (row >= 0)
    return row, col, same


def _same_block(row, col, k):
    shift = int(math.log2(k))
    return (row >> shift) == (col >> shift)


def _cumsum_rows(g, causal, same, c):
    rows = g.shape[0]
    tril = causal.astype(F32)
    big = _dot_hi(tril, g)
    big_t = lax.dot_general(g, tril, (((0,), (1,)), ((), ())), precision=HI, preferred_element_type=F32)
    tot = _dot_hi(same.astype(F32), g) if c < rows else big[rows - 1:rows, :]
    return big, big_t, tot


def _split(a):
    hi = a.astype(BF16)
    return hi, (a - hi.astype(F32)).astype(BF16)


def _unit_lower_inverses(ms, row, col, c):
    eye = (row == col).astype(F32)
    base = _same_block(row, col, 8)
    ns = [-jnp.where(base, m, 0.0) for m in ms]
    ts = [eye + n for n in ns]
    ps = [n.astype(BF16) for n in ns]
    ps = [_dot(p, p) for p in ps]
    for it in range(2):
        ps = [p.astype(BF16) for p in ps]
        ts = [t + _dot(t.astype(BF16), p) for t, p in zip(ts, ps)]
        if it == 0:
            ps = [_dot(p, p) for p in ps]
    k = 16
    while k <= c:
        off_mask = _same_block(row, col, k) & jnp.logical_not(_same_block(row, col, k // 2))
        offs = [jnp.where(off_mask, m, 0.0).astype(BF16) for m in ms]
        t16 = [t.astype(BF16) for t in ts]
        xs = [_dot(t, o).astype(BF16) for t, o in zip(t16, offs)]
        ts = [t - _dot(x, t2) for t, x, t2 in zip(ts, xs, t16)]
        k *= 2
    return ts


def _gdn_kernel(pm_ref, ps_ref, cb_ref, s0_ref, cw_ref, alog_ref, dtb_ref, nw_ref,
                og_ref, ncb_ref, s_ref, xp_ref, y_ref, *, c, nseq):
    rows = c * nseq

    @pl.when(pl.program_id(1) == 0)
    def _():
        xp_ref[:, CONV_PAD - 3:CONV_PAD, :] = cb_ref[...]
        s_ref[...] = s0_ref[...]

    cw = cw_ref[...]
    for s in range(nseq):
        rs = slice(s * c, (s + 1) * c)
        y_s, tail = _causal_conv(pm_ref[rs, :D_CONV_A], xp_ref.at[s], cw, c)
        ncb_ref[s] = tail
        y_ref[rs, :] = _silu(y_s)

    row, col, same = _seq_masks(rows, c)
    causal = same & (row >= col)
    strict = same & (row > col)
    sm = ps_ref[...]
    g = -jnp.exp(alog_ref[...]) * _softplus(sm + dtb_ref[...])
    beta_all = jax.nn.sigmoid(sm)
    big, big_t, tot = _cumsum_rows(g, causal, same, c)
    nw = nw_ref[...]
    heads = range(H_A)

    ms, attn, rhs, qe, kd = [], [], [], [], []
    for h in heads:
        q = y_ref[:, h * DK_A:(h + 1) * DK_A]
        k = y_ref[:, D_QK_A + h * DK_A:D_QK_A + (h + 1) * DK_A]
        v = y_ref[:, 2 * D_QK_A + h * DV_A:2 * D_QK_A + (h + 1) * DV_A]
        q = q * lax.rsqrt(jnp.sum(q * q, axis=-1, keepdims=True) + EPS) * (DK_A ** -0.5)
        k = k * lax.rsqrt(jnp.sum(k * k, axis=-1, keepdims=True) + EPS)
        gc = big[:, h:h + 1]
        gr = big_t[h:h + 1, :]
        beta = beta_all[:, H_A + h:H_A + h + 1]
        gamma = jnp.exp(jnp.where(causal, gc - gr, -jnp.inf))
        eg = jnp.exp(gc)
        kb = k * beta
        k16 = k.astype(BF16)
        ms.append(jnp.where(strict, _dot_nt(kb.astype(BF16), k16) * gamma, 0.0))
        attn.append((_dot_nt(q.astype(BF16), k16) * gamma).astype(BF16))
        rhs.append(jnp.concatenate([v * beta, kb * eg], axis=1).astype(BF16))
        qe.append(q * eg)
        kd.append(k * jnp.exp(tot[:, h:h + 1] - gc))

    ts = _unit_lower_inverses(ms, row, col, c)
    sols = [_dot(t.astype(BF16), r) for t, r in zip(ts, rhs)]

    for h in heads:
        u, w = sols[h][:, :DV_A], sols[h][:, DV_A:]
        vn, oq = [], []
        for s in range(nseq):
            rs = slice(s * c, (s + 1) * c)
            s16 = s_ref[s, h].astype(BF16)
            vn.append(u[rs] - _dot(w[rs].astype(BF16), s16))
            oq.append(_dot(qe[h][rs].astype(BF16), s16))
        vn16 = [a.astype(BF16) for a in vn]
        o = (oq[0] if nseq == 1 else jnp.concatenate(oq, axis=0)) + _dot(
            attn[h], vn16[0] if nseq == 1 else jnp.concatenate(vn, axis=0).astype(BF16))
        for s in range(nseq):
            rs = slice(s * c, (s + 1) * c)
            decay = jnp.exp(tot[s * c:s * c + 1, h:h + 1] if nseq > 1 else tot[:, h:h + 1])
            s_ref[s, h] = s_ref[s, h] * decay + _dot_tn(kd[h][rs].astype(BF16), vn16[s])
        gate = pm_ref[:, D_CONV_A + h * DV_A:D_CONV_A + (h + 1) * DV_A]
        og_ref[:, h * DV_A:(h + 1) * DV_A] = (_rms(o, nw) * _silu(gate)).astype(og_ref.dtype)


def _ssd_kernel(pm_ref, ps_ref, cb_ref, s0_ref, cw_ref, cbias_ref, dtb_ref, alog_ref, dskip_ref, nw_ref,
                og_ref, ncb_ref, s_ref, xp_ref, y_ref, *, c, nseq):
    assert nseq == 1
    s_ref = s_ref.at[0]

    @pl.when(pl.program_id(1) == 0)
    def _():
        xp_ref[:, CONV_PAD - 3:CONV_PAD, :] = cb_ref[...]
        s_ref[...] = s0_ref[0]

    y, tail = _causal_conv(pm_ref[:, D_INNER_C:D_INNER_C + D_XBC_C], xp_ref.at[0], cw_ref[...], c)
    ncb_ref[0] = tail
    y = _silu(y + cbias_ref[...])
    x = y[:, :D_INNER_C]

    dt = _softplus(ps_ref[...] + dtb_ref[...])
    da = dt * (-jnp.exp(alog_ref[...]))
    row, col, same = _seq_masks(c, c)
    causal = row >= col
    big, big_t, g_last = _cumsum_rows(da, causal, same, c)
    e_row = lax.broadcasted_iota(jnp.int32, (LANES, D_INNER_C), 0)
    e_col = lax.broadcasted_iota(jnp.int32, (LANES, D_INNER_C), 1)
    expand = (e_col // P_C == e_row).astype(BF16)

    def expand_heads(a):
        hi, lo = _split(a)
        return _dot(hi, expand) + _dot(lo, expand)

    dt_e = expand_heads(dt)
    eg_e = expand_heads(jnp.exp(big))
    dec_e = expand_heads(jnp.exp(g_last - big))
    xdt = x * dt_e
    xdt16 = xdt.astype(BF16)
    xdec16 = (xdt * dec_e).astype(BF16)

    for g in range(G_C):
        bg = y[:, D_INNER_C + g * N_C:D_INNER_C + (g + 1) * N_C].astype(BF16)
        cg = y[:, D_INNER_C + (G_C + g) * N_C:D_INNER_C + (G_C + g + 1) * N_C].astype(BF16)
        gs = slice(g * R_C * P_C, (g + 1) * R_C * P_C)
        cbm = _dot_nt(cg, bg)
        s_g = s_ref[g * R_C:(g + 1) * R_C].reshape(R_C * P_C, N_C)
        y_off = _dot_nt(cg, s_g.astype(BF16)) * eg_e[:, gs]
        st = _dot_tn(xdec16[:, gs], bg)
        for r in range(R_C):
            h = g * R_C + r
            seg = jnp.exp(jnp.where(causal, big[:, h:h + 1] - big_t[h:h + 1, :], -jnp.inf))
            yd = _dot((cbm * seg).astype(BF16), xdt16[:, h * P_C:(h + 1) * P_C])
            y_ref[:, h * P_C:(h + 1) * P_C] = yd + y_off[:, r * P_C:(r + 1) * P_C]
            s_ref[h] = s_ref[h] * jnp.exp(g_last[:, h:h + 1]) + st[r * P_C:(r + 1) * P_C, :]

    zg = pm_ref[:, :D_INNER_C]
    yg = (y_ref[...] + x * dskip_ref[...]) * _silu(zg)
    nw = nw_ref[...]
    width = D_INNER_C // G_C
    for g in range(G_C):
        gs = slice(g * width, (g + 1) * width)
        og_ref[:, gs] = _rms(yg[:, gs], nw[:, gs]).astype(og_ref.dtype)


def _seq_call(body, pm, ps, conv_buf, s0, params, *, n_seq, seq_len, c, nseq, row_off, d_out, o_dtype, d_conv,
              d_scratch):
    nz = seq_len // c
    rows = nseq * c
    assert nseq == 1 or nz == 1
    assert n_seq % nseq == 0 and row_off % rows == 0
    blk_off = row_off // rows
    state_block = (nseq,) + s0.shape[1:]
    n_state = len(s0.shape) - 1

    def const(a):
        return pl.BlockSpec(a.shape, lambda b, z: (0,) * a.ndim)

    return pl.pallas_call(
        functools.partial(body, c=c, nseq=nseq),
        grid=(n_seq // nseq, nz),
        in_specs=[pl.BlockSpec((rows, pm.shape[1]), lambda b, z: (blk_off + b * nz + z, 0)),
                  pl.BlockSpec((rows, ps.shape[1]), lambda b, z: (blk_off + b * nz + z, 0)),
                  pl.BlockSpec((nseq, CONV_W - 1, d_conv), lambda b, z: (b, 0, 0)),
                  pl.BlockSpec(state_block, lambda b, z: (b,) + (0,) * n_state)]
                 + [const(a) for a in params],
        out_specs=[pl.BlockSpec((rows, d_out), lambda b, z: (b * nz + z, 0)),
                   pl.BlockSpec((nseq, CONV_W - 1, d_conv), lambda b, z: (b, 0, 0)),
                   pl.BlockSpec(state_block, lambda b, z: (b,) + (0,) * n_state)],
        out_shape=[jax.ShapeDtypeStruct((n_seq * seq_len, d_out), o_dtype),
                   jax.ShapeDtypeStruct((n_seq, CONV_W - 1, d_conv), F32),
                   jax.ShapeDtypeStruct(s0.shape, F32)],
        scratch_shapes=[pltpu.VMEM((nseq, CONV_PAD + c, d_conv), F32), pltpu.VMEM((rows, d_scratch), F32)],
        compiler_params=pltpu.CompilerParams(dimension_semantics=("arbitrary", "arbitrary"),
                                             vmem_limit_bytes=VMEM_LIMIT_BYTES),
    )(pm, ps, conv_buf, s0, *params)


def _pad_lanes(a):
    a = a.reshape(1, -1).astype(F32)
    return jnp.pad(a, ((0, 0), (0, LANES - a.shape[1])))


def _split_in_proj(w_in, n_main):
    wm = w_in[:, :n_main].astype(BF16)
    ws = jnp.pad(w_in[:, n_main:], ((0, 0), (0, LANES - (w_in.shape[1] - n_main)))).astype(BF16)
    return wm, ws


def kernel(x_prompt, x_sample, state_gdn, state_gdn_conv, state_ssd, state_ssd_conv, norm_w, ffn_w_in, ffn_w_out, gdn_w_in, gdn_conv_w, gdn_a_log, gdn_dt_bias, gdn_norm_w, gdn_w_out, cmlp_w_in, cmlp_b_in, cmlp_ln_w, cmlp_ln_b, cmlp_w_s, cmlp_b_s, cmlp_w_out, ssd_w_in, ssd_conv_w, ssd_conv_b, ssd_dt_bias, ssd_a_log, ssd_d, ssd_norm_w, ssd_w_out):
    nb, seq, _ = x_prompt.shape
    ndb, dseq, _ = x_sample.shape
    depth = norm_w.shape[0]
    n_p, n_s = nb * seq, ndb * dseq
    x = (x_prompt.reshape(n_p, D_MODEL), x_sample.reshape(n_s, D_MODEL))

    gdn_p, gdn_conv_p, ssd_p, ssd_conv_p = [], [], [], []
    gdn_s, gdn_conv_s, ssd_s, ssd_conv_s, cmlp_s = [], [], [], [], []
    for i in range(depth):
        kind, j, last = i % 3, i // 3, i == depth - 1
        nw = jnp.pad(norm_w[i], ((0, 2), (0, 0)))
        wi1, wo1 = ffn_w_in[i, 0].astype(BF16), ffn_w_out[i, 0].astype(BF16)
        wi2, wo2 = ffn_w_in[i, 1].astype(BF16), ffn_w_out[i, 1].astype(BF16)
        if kind == 0:
            wm, ws = _split_in_proj(gdn_w_in[j], D_CONV_A + D_V_A)
            x1, pm, ps = _pre_call(x, n_p, n_s, nw, wi1, wo1, wm, ws)
            params = (gdn_conv_w[j], _pad_lanes(gdn_a_log[j]), _pad_lanes(gdn_dt_bias[j]),
                      gdn_norm_w[j].reshape(1, DV_A))
            kw = dict(d_out=D_V_A, o_dtype=BF16, d_conv=D_CONV_A, d_scratch=D_CONV_A)
            o_p, cb_p, st_p = _seq_call(
                _gdn_kernel, pm, ps, jnp.zeros((nb, CONV_W - 1, D_CONV_A), F32),
                jnp.zeros((nb, H_A, DK_A, DV_A), F32), params,
                n_seq=nb, seq_len=seq, c=min(CHUNK_A, seq), nseq=1, row_off=0, **kw)
            c_s = min(CHUNK_A, dseq)
            o_s, cb_s, st_s = _seq_call(
                _gdn_kernel, pm, ps, state_gdn_conv[j], state_gdn[j], params,
                n_seq=ndb, seq_len=dseq, c=c_s, nseq=CHUNK_A // c_s if dseq == c_s else 1, row_off=n_p, **kw)
            gdn_p.append(st_p); gdn_conv_p.append(cb_p); gdn_s.append(st_s); gdn_conv_s.append(cb_s)
            x = _post_call(x1, (o_p, o_s), n_p, n_s, gdn_w_out[j].astype(BF16), nw, wi2, wo2, pair_out=last)
        elif kind == 1:
            x1, u, v = _pre_cmlp_call(x, n_p, n_s, nw, wi1, wo1, cmlp_w_in[j].astype(BF16),
                                      cmlp_b_in[j].reshape(1, -1), cmlp_ln_w[j].reshape(1, -1),
                                      cmlp_ln_b[j].reshape(1, -1))
            cmlp_s.append(v[n_p:].reshape(ndb, dseq, D_B))
            reps = CHUNK_B // dseq
            w_s = cmlp_w_s[j]
            w_blk = jax.vmap(lambda m: jnp.kron(jnp.eye(reps, dtype=F32), m))(w_s[:, :dseq, :dseq])
            wmix = jnp.stack([w_s, w_blk])
            b_t = cmlp_b_s[j].T
            bmix = jnp.stack([b_t, jnp.tile(b_t[:dseq], (reps, 1))])
            bmix = jnp.pad(bmix, ((0, 0), (0, 0), (0, LANES - H_B)))
            x = _post_cmlp_call(x1, u, v, n_p, n_s, wmix, bmix, cmlp_w_out[j].astype(BF16), nw, wi2, wo2,
                                pair_out=last)
        else:
            wm, ws = _split_in_proj(ssd_w_in[j], D_INNER_C + D_XBC_C)
            x1, pm, ps = _pre_call(x, n_p, n_s, nw, wi1, wo1, wm, ws)
            params = (ssd_conv_w[j], ssd_conv_b[j].reshape(1, -1), _pad_lanes(ssd_dt_bias[j]),
                      _pad_lanes(ssd_a_log[j]), jnp.repeat(ssd_d[j], P_C).reshape(1, -1),
                      ssd_norm_w[j].reshape(1, -1))
            kw = dict(d_out=D_INNER_C, d_conv=D_XBC_C, d_scratch=D_INNER_C, nseq=1)
            o_p, cb_p, st_p = _seq_call(
                _ssd_kernel, pm, ps, jnp.zeros((nb, CONV_W - 1, D_XBC_C), F32),
                jnp.zeros((nb, H_C, P_C, N_C), F32), params,
                n_seq=nb, seq_len=seq, c=min(CHUNK_C, seq), row_off=0, o_dtype=BF16, **kw)
            o_s, cb_s, st_s = _seq_call(
                _ssd_kernel, pm, ps, state_ssd_conv[j], state_ssd[j], params,
                n_seq=ndb, seq_len=dseq, c=min(CHUNK_C, dseq), row_off=n_p, o_dtype=F32, **kw)
            ssd_p.append(st_p); ssd_conv_p.append(cb_p); ssd_s.append(st_s); ssd_conv_s.append(cb_s)
            x = _post_call(x1, (o_p, o_s), n_p, n_s, ssd_w_out[j].astype(BF16), nw, wi2, wo2, pair_out=last)

    return (x[0].reshape(nb, seq, D_MODEL), x[1].reshape(ndb, dseq, D_MODEL),
            jnp.stack(gdn_p), jnp.stack(gdn_conv_p), jnp.stack(ssd_p), jnp.stack(ssd_conv_p),
            jnp.stack(gdn_s), jnp.stack(gdn_conv_s), jnp.stack(ssd_s), jnp.stack(ssd_conv_s),
            jnp.stack(cmlp_s))
```

```python
import functools
import math

import jax
import jax.numpy as jnp
from jax import lax
from jax.experimental import pallas as pl
from jax.experimental.pallas import tpu as pltpu

F32 = jnp.float32
BF16 = jnp.bfloat16
HI = lax.Precision.HIGHEST

D_MODEL = 1024
D_FF = 2816
EPS = 1e-6
CONV_W = 4
H_A, DK_A, DV_A = 8, 128, 128
D_QK_A, D_V_A = H_A * DK_A, H_A * DV_A
D_CONV_A = 2 * D_QK_A + D_V_A
CHUNK_A = 64
CHUNK_B, D_B, H_B = 128, 2 * D_MODEL, 8
DG_B = D_B // H_B
D_INNER_C, P_C, G_C, N_C = 2 * D_MODEL, 64, 4, 128
H_C = D_INNER_C // P_C
R_C = H_C // G_C
D_XBC_C = D_INNER_C + 2 * G_C * N_C
CHUNK_C = 128

LANES = 128
ROW_TILE = 256
VMEM_LIMIT_BYTES = 60 * 1024 * 1024
CONV_PAD = 8
PROJ_CHUNK = 1024
GDN_PROMPT_GROUPS = 4
SSD_STACK_ROWS = 64

_NT = (((1,), (1,)), ((), ()))
_TN = (((0,), (0,)), ((), ()))


def _dot(a, b):
    return jnp.dot(a, b, preferred_element_type=F32)


def _dot_hi(a, b):
    return jnp.dot(a, b, precision=HI, preferred_element_type=F32)


def _dot_nt(a, b):
    return lax.dot_general(a, b, _NT, preferred_element_type=F32)


def _dot_tn(a, b):
    return lax.dot_general(a, b, _TN, preferred_element_type=F32)


def _silu(x):
    return x * jax.nn.sigmoid(x)


def _softplus(x):
    return jnp.maximum(x, 0.0) + jnp.log1p(jnp.exp(-jnp.abs(x)))


def _rms(x, w):
    return x * lax.rsqrt(jnp.mean(x * x, axis=-1, keepdims=True) + EPS) * w


def _half_ffn(x, w_pre, w_post, wi_ref, wo_ref):
    h = _rms(x, w_pre).astype(BF16)
    gu = _dot(h, wi_ref[...])
    a = (_silu(gu[:, :D_FF]) * gu[:, D_FF:]).astype(BF16)
    return x + 0.5 * _rms(_dot(a, wo_ref[...]), w_post)


def _dot_into(out_ref, h, w_ref):
    width = w_ref.shape[1]
    for c0 in range(0, width, PROJ_CHUNK):
        c1 = min(c0 + PROJ_CHUNK, width)
        out_ref[:, c0:c1] = _dot(h, w_ref[:, c0:c1])


def _read_rows(refs, is_p, dtype=None):
    vals = [r[...] if dtype is None else r[...].astype(dtype) for r in refs]
    return vals[0] if len(vals) == 1 else jnp.where(is_p, vals[0], vals[1])


def _write_rows(refs, is_p, val):
    if len(refs) == 1:
        refs[0][...] = val
        return

    @pl.when(is_p)
    def _():
        refs[0][...] = val

    @pl.when(jnp.logical_not(is_p))
    def _():
        refs[1][...] = val


def _pre_kernel(*refs, npt, n_x):
    x_refs = refs[:n_x]
    nw_ref, wi_ref, wo_ref, wm_ref, ws_ref, x1_ref, pm_ref, ps_ref = refs[n_x:]
    is_p = pl.program_id(0) < npt
    nw = nw_ref[...]
    x1 = _half_ffn(_read_rows(x_refs, is_p), nw[0:1], nw[1:2], wi_ref, wo_ref)
    x1_ref[...] = x1
    h = _rms(x1, nw[2:3]).astype(BF16)
    _dot_into(pm_ref, h, wm_ref)
    ps_ref[...] = _dot(h, ws_ref[...])


def _pre_cmlp_kernel(*refs, npt, n_x):
    x_refs = refs[:n_x]
    nw_ref, wi_ref, wo_ref, wm_ref, b_ref, lnw_ref, lnb_ref, x1_ref, u_ref, v_ref = refs[n_x:]
    is_p = pl.program_id(0) < npt
    nw = nw_ref[...]
    x1 = _half_ffn(_read_rows(x_refs, is_p), nw[0:1], nw[1:2], wi_ref, wo_ref)
    x1_ref[...] = x1
    h = _rms(x1, nw[2:3]).astype(BF16)
    p = _dot(h, wm_ref[...]) + b_ref[...]
    ge = 0.5 * p * (1.0 + lax.erf(p * (1.0 / math.sqrt(2.0))))
    u_ref[...] = ge[:, :D_B]
    v = ge[:, D_B:]
    mu = jnp.mean(v, axis=-1, keepdims=True)
    vc = v - mu
    var = jnp.mean(vc * vc, axis=-1, keepdims=True)
    v_ref[...] = vc * lax.rsqrt(var + EPS) * lnw_ref[...] + lnb_ref[...]


def _post_kernel(x1_ref, op_ref, os_ref, wom_ref, nw_ref, wi_ref, wo_ref, *x3_refs, npt):
    is_p = pl.program_id(0) < npt
    nw = nw_ref[...]
    m = _dot(_read_rows((op_ref, os_ref), is_p, BF16), wom_ref[...])
    x2 = x1_ref[...] + _rms(m, nw[3:4])
    _write_rows(x3_refs, is_p, _half_ffn(x2, nw[4:5], nw[5:6], wi_ref, wo_ref))


def _post_cmlp_kernel(x1_ref, u_ref, v_ref, wmix_ref, bmix_ref, wom_ref, nw_ref, wi_ref, wo_ref, *x3_and_scratch,
                      npt):
    x3_refs, a_ref = x3_and_scratch[:-1], x3_and_scratch[-1]
    is_p = pl.program_id(0) < npt
    nw = nw_ref[...]
    row = lax.broadcasted_iota(jnp.int32, (CHUNK_B, CHUNK_B), 0)
    col = lax.broadcasted_iota(jnp.int32, (CHUNK_B, CHUNK_B), 1)
    bmix = bmix_ref[...]
    for h in range(H_B):
        w = jnp.where(row >= col, wmix_ref[h], 0.0).astype(BF16)
        cs = slice(h * DG_B, (h + 1) * DG_B)
        for r in range(ROW_TILE // CHUNK_B):
            rs = slice(r * CHUNK_B, (r + 1) * CHUNK_B)
            mixed = _dot(w, v_ref[rs, cs].astype(BF16)) + bmix[:, h:h + 1]
            a_ref[rs, cs] = (u_ref[rs, cs] * mixed).astype(BF16)
    m = _dot(a_ref[...], wom_ref[...])
    x2 = x1_ref[...] + _rms(m, nw[3:4])
    _write_rows(x3_refs, is_p, _half_ffn(x2, nw[4:5], nw[5:6], wi_ref, wo_ref))


def _resident(a):
    nd = a.ndim
    return pl.BlockSpec(a.shape, lambda i: (0,) * nd, pipeline_mode=pl.Buffered(1))


def _rows(width):
    return pl.BlockSpec((ROW_TILE, width), lambda i: (i, 0))


def _pair_specs(width, npt):
    return [pl.BlockSpec((ROW_TILE, width), lambda i: (jnp.minimum(i, npt - 1), 0)),
            pl.BlockSpec((ROW_TILE, width), lambda i: (jnp.maximum(i - npt, 0), 0))]


def _x_specs(x, npt):
    return _pair_specs(D_MODEL, npt) if isinstance(x, tuple) else [_rows(D_MODEL)]


def _x_out(n_p, n_s, npt, pair):
    if pair:
        return _pair_specs(D_MODEL, npt), [jax.ShapeDtypeStruct((n_p, D_MODEL), F32),
                                           jax.ShapeDtypeStruct((n_s, D_MODEL), F32)]
    return [_rows(D_MODEL)], [jax.ShapeDtypeStruct((n_p + n_s, D_MODEL), F32)]


def _as_list(x):
    return list(x) if isinstance(x, tuple) else [x]


def _row_call(body, n_p, n_s, in_arrays, in_specs, out_specs, out_shape, scratch_shapes=()):
    assert n_p % ROW_TILE == 0 and n_s % ROW_TILE == 0
    return pl.pallas_call(
        body,
        grid=((n_p + n_s) // ROW_TILE,),
        in_specs=in_specs,
        out_specs=out_specs,
        out_shape=out_shape,
        scratch_shapes=list(scratch_shapes),
        compiler_params=pltpu.CompilerParams(dimension_semantics=("arbitrary",),
                                             vmem_limit_bytes=VMEM_LIMIT_BYTES),
    )(*in_arrays)


def _pre_call(x, n_p, n_s, nw, wi, wo, wm, ws):
    npt = n_p // ROW_TILE
    xs = _as_list(x)
    widths = (D_MODEL, wm.shape[1], ws.shape[1])
    return _row_call(
        functools.partial(_pre_kernel, npt=npt, n_x=len(xs)), n_p, n_s, (*xs, nw, wi, wo, wm, ws),
        _x_specs(x, npt) + [_resident(a) for a in (nw, wi, wo, wm, ws)],
        [_rows(w) for w in widths], [jax.ShapeDtypeStruct((n_p + n_s, w), F32) for w in widths])


def _pre_cmlp_call(x, n_p, n_s, nw, wi, wo, wm, b, lnw, lnb):
    npt = n_p // ROW_TILE
    xs = _as_list(x)
    widths = (D_MODEL, D_B, D_B)
    return _row_call(
        functools.partial(_pre_cmlp_kernel, npt=npt, n_x=len(xs)), n_p, n_s, (*xs, nw, wi, wo, wm, b, lnw, lnb),
        _x_specs(x, npt) + [_resident(a) for a in (nw, wi, wo, wm, b, lnw, lnb)],
        [_rows(w) for w in widths], [jax.ShapeDtypeStruct((n_p + n_s, w), F32) for w in widths])


def _post_call(x1, o, n_p, n_s, wom, nw, wi, wo, *, pair_out):
    npt = n_p // ROW_TILE
    out_specs, out_shape = _x_out(n_p, n_s, npt, pair_out)
    out = _row_call(
        functools.partial(_post_kernel, npt=npt), n_p, n_s, (x1, *o, wom, nw, wi, wo),
        [_rows(D_MODEL)] + _pair_specs(o[0].shape[1], npt) + [_resident(a) for a in (wom, nw, wi, wo)],
        out_specs, out_shape)
    return tuple(out) if pair_out else out[0]


def _post_cmlp_call(x1, u, v, n_p, n_s, wmix, bmix, wom, nw, wi, wo, *, pair_out):
    npt = n_p // ROW_TILE
    out_specs, out_shape = _x_out(n_p, n_s, npt, pair_out)
    out = _row_call(
        functools.partial(_post_cmlp_kernel, npt=npt), n_p, n_s, (x1, u, v, wmix, bmix, wom, nw, wi, wo),
        [_rows(D_MODEL), _rows(D_B), _rows(D_B),
         pl.BlockSpec((None, H_B, CHUNK_B, CHUNK_B), lambda i: (i // npt, 0, 0, 0)),
         pl.BlockSpec((None, CHUNK_B, LANES), lambda i: (i // npt, 0, 0))]
        + [_resident(a) for a in (wom, nw, wi, wo)],
        out_specs, out_shape, scratch_shapes=[pltpu.VMEM((ROW_TILE, D_B), BF16)])
    return tuple(out) if pair_out else out[0]


def _causal_conv(x, xp_ref, cw, c):
    xp_ref[CONV_PAD:CONV_PAD + c, :] = x
    y = cw[3:4] * x
    for j in range(CONV_W - 1):
        y = y + cw[j:j + 1] * xp_ref[CONV_PAD - 3 + j:CONV_PAD - 3 + j + c, :]
    tail = xp_ref[CONV_PAD + c - 3:CONV_PAD + c, :]
    xp_ref[CONV_PAD - 3:CONV_PAD, :] = tail
    return y, tail


def _seq_masks(rows, c):
    row = lax.broadcasted_iota(jnp.int32, (rows, rows), 0)
    col = lax.broadcasted_iota(jnp.int32, (rows, rows), 1)
    same = _same_block(row, col, c) if c < rows else (row >= 0)
    return row, col, same


def _same_block(row, col, k):
    shift = int(math.log2(k))
    return (row >> shift) == (col >> shift)


def _cumsum_rows(g, causal, same, c):
    rows = g.shape[0]
    tril = causal.astype(F32)
    big = _dot_hi(tril, g)
    big_t = lax.dot_general(g, tril, (((0,), (1,)), ((), ())), precision=HI, preferred_element_type=F32)
    tot = _dot_hi(same.astype(F32), g) if c < rows else big[rows - 1:rows, :]
    return big, big_t, tot


def _split(a):
    hi = a.astype(BF16)
    return hi, (a - hi.astype(F32)).astype(BF16)


def _unit_lower_inverses(ms, row, col, c):
    eye = (row == col).astype(F32)
    base = _same_block(row, col, 8)
    ns = [-jnp.where(base, m, 0.0) for m in ms]
    ts = [eye + n for n in ns]
    ps = [n.astype(BF16) for n in ns]
    ps = [_dot(p, p) for p in ps]
    for it in range(2):
        ps = [p.astype(BF16) for p in ps]
        ts = [t + _dot(t.astype(BF16), p) for t, p in zip(ts, ps)]
        if it == 0:
            ps = [_dot(p, p) for p in ps]
    k = 16
    while k <= c:
        off_mask = _same_block(row, col, k) & jnp.logical_not(_same_block(row, col, k // 2))
        offs = [jnp.where(off_mask, m, 0.0).astype(BF16) for m in ms]
        t16 = [t.astype(BF16) for t in ts]
        xs = [_dot(t, o).astype(BF16) for t, o in zip(t16, offs)]
        ts = [t - _dot(x, t2) for t, x, t2 in zip(ts, xs, t16)]
        k *= 2
    return ts


def _seq_refs(refs, nbat, n_prev, n_params):
    pm_refs, ps_refs = refs[:nbat], refs[nbat:2 * nbat]
    i = 2 * nbat
    cb_ref, s0_ref = refs[i:i + 2]
    i += 2
    prev_ref = refs[i] if n_prev else None
    i += 1 if n_prev else 0
    params = refs[i:i + n_params]
    og_ref, ncb_ref, sall_ref, xp_ref, y_ref = refs[i + n_params:]

    @pl.when(pl.program_id(1) == 0)
    def _():
        xp_ref[:, CONV_PAD - 3:CONV_PAD, :] = cb_ref[...]
        sall_ref[n_prev] = s0_ref[...]
        if n_prev:
            sall_ref[0:n_prev] = prev_ref[...]

    return pm_refs, ps_refs, params, og_ref, ncb_ref, sall_ref.at[n_prev], xp_ref, y_ref


def _gdn_kernel(*refs, c, nseq, nbat, n_prev):
    pm_refs, ps_refs, params, og_ref, ncb_ref, s_ref, xp_ref, y_ref = _seq_refs(refs, nbat, n_prev, 4)
    cw_ref, alog_ref, dtb_ref, nw_ref = params
    rows = c * nseq
    cw = cw_ref[...]
    nw = nw_ref[...]
    row, col, same = _seq_masks(rows, c)
    causal = same & (row >= col)
    strict = same & (row > col)

    ms, attn, rhs, qe, kd, tots = [], [], [], [], [], []
    for a in range(nbat):
        for s in range(nseq):
            rs = slice(s * c, (s + 1) * c)
            y_s, tail = _causal_conv(pm_refs[a][rs, :D_CONV_A], xp_ref.at[a * nseq + s], cw, c)
            ncb_ref[a * nseq + s] = tail
            y_ref[a, rs, :] = _silu(y_s)
        sm = ps_refs[a][...]
        g = -jnp.exp(alog_ref[...]) * _softplus(sm + dtb_ref[...])
        beta_all = jax.nn.sigmoid(sm)
        big, big_t, tot = _cumsum_rows(g, causal, same, c)
        tots.append(tot)
        for h in range(H_A):
            q = y_ref[a, :, h * DK_A:(h + 1) * DK_A]
            k = y_ref[a, :, D_QK_A + h * DK_A:D_QK_A + (h + 1) * DK_A]
            v = y_ref[a, :, 2 * D_QK_A + h * DV_A:2 * D_QK_A + (h + 1) * DV_A]
            q = q * lax.rsqrt(jnp.sum(q * q, axis=-1, keepdims=True) + EPS) * (DK_A ** -0.5)
            k = k * lax.rsqrt(jnp.sum(k * k, axis=-1, keepdims=True) + EPS)
            gc = big[:, h:h + 1]
            gr = big_t[h:h + 1, :]
            beta = beta_all[:, H_A + h:H_A + h + 1]
            gamma = jnp.exp(jnp.where(causal, gc - gr, -jnp.inf))
            eg = jnp.exp(gc)
            kb = k * beta
            k16 = k.astype(BF16)
            ms.append(jnp.where(strict, _dot_nt(kb.astype(BF16), k16) * gamma, 0.0))
            attn.append((_dot_nt(q.astype(BF16), k16) * gamma).astype(BF16))
            rhs.append(jnp.concatenate([v * beta, kb * eg], axis=1).astype(BF16))
            qe.append(q * eg)
            kd.append(k * jnp.exp(tot[:, h:h + 1] - gc))

    ts = _unit_lower_inverses(ms, row, col, c)
    sols = [_dot(t.astype(BF16), r) for t, r in zip(ts, rhs)]

    for a in range(nbat):
        for h in range(H_A):
            i = a * H_A + h
            u, w = sols[i][:, :DV_A], sols[i][:, DV_A:]
            vn, oq = [], []
            for s in range(nseq):
                rs = slice(s * c, (s + 1) * c)
                s16 = s_ref[a * nseq + s, h].astype(BF16)
                vn.append(u[rs] - _dot(w[rs].astype(BF16), s16))
                oq.append(_dot(qe[i][rs].astype(BF16), s16))
            vn16 = [x.astype(BF16) for x in vn]
            o = (oq[0] if nseq == 1 else jnp.concatenate(oq, axis=0)) + _dot(
                attn[i], vn16[0] if nseq == 1 else jnp.concatenate(vn, axis=0).astype(BF16))
            for s in range(nseq):
                rs = slice(s * c, (s + 1) * c)
                decay = jnp.exp(tots[a][s * c:s * c + 1, h:h + 1] if nseq > 1 else tots[a][:, h:h + 1])
                s_ref[a * nseq + s, h] = s_ref[a * nseq + s, h] * decay + _dot_tn(
                    kd[i][rs].astype(BF16), vn16[s])
            gate = pm_refs[a][:, D_CONV_A + h * DV_A:D_CONV_A + (h + 1) * DV_A]
            og_ref[a, :, h * DV_A:(h + 1) * DV_A] = (_rms(o, nw) * _silu(gate)).astype(og_ref.dtype)


def _ssd_kernel(*refs, c, nseq, nbat, n_prev):
    assert nbat == 1
    pm_refs, ps_refs, params, og_ref, ncb_ref, s_ref, xp_ref, y_ref = _seq_refs(refs, nbat, n_prev, 6)
    cw_ref, cbias_ref, dtb_ref, alog_ref, dskip_ref, nw_ref = params
    pm_ref, ps_ref, og_ref, y_ref = pm_refs[0], ps_refs[0], og_ref.at[0], y_ref.at[0]
    rows = c * nseq

    cw, cbias = cw_ref[...], cbias_ref[...]
    ys = []
    for s in range(nseq):
        y_s, tail = _causal_conv(pm_ref[s * c:(s + 1) * c, D_INNER_C:D_INNER_C + D_XBC_C], xp_ref.at[s], cw, c)
        ncb_ref[s] = tail
        ys.append(_silu(y_s + cbias))
    y = ys[0] if nseq == 1 else jnp.concatenate(ys, axis=0)
    x = y[:, :D_INNER_C]

    dt = _softplus(ps_ref[...] + dtb_ref[...])
    da = dt * (-jnp.exp(alog_ref[...]))
    row, col, same = _seq_masks(rows, c)
    causal = same & (row >= col)
    big, big_t, g_last = _cumsum_rows(da, causal, same, c)
    e_row = lax.broadcasted_iota(jnp.int32, (LANES, D_INNER_C), 0)
    e_col = lax.broadcasted_iota(jnp.int32, (LANES, D_INNER_C), 1)
    expand = (e_col // P_C == e_row).astype(BF16)

    def expand_heads(a):
        hi, lo = _split(a)
        return _dot(hi, expand) + _dot(lo, expand)

    dt_e = expand_heads(dt)
    eg_e = expand_heads(jnp.exp(big))
    dec_e = expand_heads(jnp.exp(g_last - big))
    xdt = x * dt_e
    xdt16 = xdt.astype(BF16)

    xdec = xdt * dec_e

    for g in range(G_C):
        bg = y[:, D_INNER_C + g * N_C:D_INNER_C + (g + 1) * N_C]
        cg = y[:, D_INNER_C + (G_C + g) * N_C:D_INNER_C + (G_C + g + 1) * N_C]
        gs = slice(g * R_C * P_C, (g + 1) * R_C * P_C)
        cbm = _dot_nt(cg.astype(BF16), bg.astype(BF16))
        y_off, st = [], []
        for s in range(nseq):
            rs = slice(s * c, (s + 1) * c)
            s_g = s_ref[s, g * R_C:(g + 1) * R_C].reshape(R_C * P_C, N_C)
            y_off.append(_dot_nt(cg[rs].astype(BF16), s_g.astype(BF16)))
            st.append(_dot_tn(xdec[rs, gs].astype(BF16), bg[rs].astype(BF16)))
        y_off = (y_off[0] if nseq == 1 else jnp.concatenate(y_off, axis=0)) * eg_e[:, gs]
        for r in range(R_C):
            h = g * R_C + r
            seg = jnp.exp(jnp.where(causal, big[:, h:h + 1] - big_t[h:h + 1, :], -jnp.inf))
            yd = _dot((cbm * seg).astype(BF16), xdt16[:, h * P_C:(h + 1) * P_C])
            y_ref[:, h * P_C:(h + 1) * P_C] = yd + y_off[:, r * P_C:(r + 1) * P_C]
            for s in range(nseq):
                decay = jnp.exp(g_last[s * c:s * c + 1, h:h + 1] if nseq > 1 else g_last[:, h:h + 1])
                s_ref[s, h] = s_ref[s, h] * decay + st[s][r * P_C:(r + 1) * P_C, :]

    zg = pm_ref[:, :D_INNER_C]
    yg = (y_ref[...] + x * dskip_ref[...]) * _silu(zg)
    nw = nw_ref[...]
    width = D_INNER_C // G_C
    for g in range(G_C):
        gs = slice(g * width, (g + 1) * width)
        og_ref[:, gs] = _rms(yg[:, gs], nw[:, gs]).astype(og_ref.dtype)


def _seq_call(body, pm, ps, conv_buf, s0, prev, params, *, layer, n_seq, seq_len, c, nseq, nbat, row_off, d_out,
              o_dtype, d_scratch):
    nz = seq_len // c
    rows = nseq * c
    per_step = nseq * nbat
    assert nseq == 1 or nz == 1
    assert n_seq % per_step == 0 and row_off % rows == 0
    blk_off = row_off // rows
    d_conv = conv_buf.shape[-1]
    state_shape = s0.shape[2:]
    n_prev = 0 if prev is None else prev.shape[0]
    zeros = (0,) * len(state_shape)

    def const(a):
        return pl.BlockSpec(a.shape, lambda b, z: (0,) * a.ndim)

    def row_blocks(width):
        return [pl.BlockSpec((rows, width), lambda b, z, a=a: (blk_off + (b * nbat + a) * nz + z, 0))
                for a in range(nbat)]

    prev_in = [] if prev is None else [prev]
    prev_specs = [] if prev is None else [
        pl.BlockSpec((n_prev, per_step) + state_shape, lambda b, z: (0, b) + zeros)]
    og, ncb, states = pl.pallas_call(
        functools.partial(body, c=c, nseq=nseq, nbat=nbat, n_prev=n_prev),
        grid=(n_seq // per_step, nz),
        in_specs=row_blocks(pm.shape[1]) + row_blocks(ps.shape[1])
        + [pl.BlockSpec((None, per_step, CONV_W - 1, d_conv), lambda b, z: (layer, b, 0, 0)),
           pl.BlockSpec((None, per_step) + state_shape, lambda b, z: (layer, b) + zeros)]
        + prev_specs + [const(a) for a in params],
        out_specs=[pl.BlockSpec((nbat, rows, d_out), lambda b, z: (b, z, 0)),
                   pl.BlockSpec((per_step, CONV_W - 1, d_conv), lambda b, z: (b, 0, 0)),
                   pl.BlockSpec((n_prev + 1, per_step) + state_shape, lambda b, z: (0, b) + zeros)],
        out_shape=[jax.ShapeDtypeStruct((n_seq // nseq, nz * rows, d_out), o_dtype),
                   jax.ShapeDtypeStruct((n_seq, CONV_W - 1, d_conv), F32),
                   jax.ShapeDtypeStruct((n_prev + 1, n_seq) + state_shape, F32)],
        scratch_shapes=[pltpu.VMEM((per_step, CONV_PAD + c, d_conv), F32),
                        pltpu.VMEM((nbat, rows, d_scratch), F32)],
        compiler_params=pltpu.CompilerParams(dimension_semantics=("arbitrary", "arbitrary"),
                                             vmem_limit_bytes=VMEM_LIMIT_BYTES),
    )(*([pm] * nbat), *([ps] * nbat), conv_buf, s0, *prev_in, *params)
    return og.reshape(n_seq * seq_len, d_out), ncb, states


def _pad_lanes(a):
    a = a.reshape(1, -1).astype(F32)
    return jnp.pad(a, ((0, 0), (0, LANES - a.shape[1])))


def _split_in_proj(w_in, n_main):
    wm = w_in[:, :n_main].astype(BF16)
    ws = jnp.pad(w_in[:, n_main:], ((0, 0), (0, LANES - (w_in.shape[1] - n_main)))).astype(BF16)
    return wm, ws


def kernel(x_prompt, x_sample, state_gdn, state_gdn_conv, state_ssd, state_ssd_conv, norm_w, ffn_w_in, ffn_w_out, gdn_w_in, gdn_conv_w, gdn_a_log, gdn_dt_bias, gdn_norm_w, gdn_w_out, cmlp_w_in, cmlp_b_in, cmlp_ln_w, cmlp_ln_b, cmlp_w_s, cmlp_b_s, cmlp_w_out, ssd_w_in, ssd_conv_w, ssd_conv_b, ssd_dt_bias, ssd_a_log, ssd_d, ssd_norm_w, ssd_w_out):
    nb, seq, _ = x_prompt.shape
    ndb, dseq, _ = x_sample.shape
    depth = norm_w.shape[0]
    n_p, n_s = nb * seq, ndb * dseq
    x = (x_prompt.reshape(n_p, D_MODEL), x_sample.reshape(n_s, D_MODEL))

    gdn_p = gdn_s = ssd_p = ssd_s = None
    gdn_conv_p, ssd_conv_p, gdn_conv_s, ssd_conv_s, cmlp_s = [], [], [], [], []
    for i in range(depth):
        kind, j, last = i % 3, i // 3, i == depth - 1
        nw = jnp.pad(norm_w[i], ((0, 2), (0, 0)))
        wi1, wo1 = ffn_w_in[i, 0].astype(BF16), ffn_w_out[i, 0].astype(BF16)
        wi2, wo2 = ffn_w_in[i, 1].astype(BF16), ffn_w_out[i, 1].astype(BF16)
        if kind == 0:
            wm, ws = _split_in_proj(gdn_w_in[j], D_CONV_A + D_V_A)
            x1, pm, ps = _pre_call(x, n_p, n_s, nw, wi1, wo1, wm, ws)
            params = (gdn_conv_w[j], _pad_lanes(gdn_a_log[j]), _pad_lanes(gdn_dt_bias[j]),
                      gdn_norm_w[j].reshape(1, DV_A))
            kw = dict(d_out=D_V_A, o_dtype=BF16, d_scratch=D_CONV_A)
            o_p, cb_p, gdn_p = _seq_call(
                _gdn_kernel, pm, ps, jnp.zeros((1, nb, CONV_W - 1, D_CONV_A), F32),
                jnp.zeros((1, nb, H_A, DK_A, DV_A), F32), gdn_p, params, layer=0,
                n_seq=nb, seq_len=seq, c=min(CHUNK_A, seq), nseq=1, nbat=math.gcd(nb, GDN_PROMPT_GROUPS),
                row_off=0, **kw)
            c_s = min(CHUNK_A, dseq)
            o_s, cb_s, gdn_s = _seq_call(
                _gdn_kernel, pm, ps, state_gdn_conv, state_gdn, gdn_s, params, layer=j,
                n_seq=ndb, seq_len=dseq, c=c_s, nseq=CHUNK_A // c_s if dseq == c_s else 1, nbat=1, row_off=n_p,
                **kw)
            gdn_conv_p.append(cb_p); gdn_conv_s.append(cb_s)
            x = _post_call(x1, (o_p, o_s), n_p, n_s, gdn_w_out[j].astype(BF16), nw, wi2, wo2, pair_out=last)
        elif kind == 1:
            x1, u, v = _pre_cmlp_call(x, n_p, n_s, nw, wi1, wo1, cmlp_w_in[j].astype(BF16),
                                      cmlp_b_in[j].reshape(1, -1), cmlp_ln_w[j].reshape(1, -1),
                                      cmlp_ln_b[j].reshape(1, -1))
            cmlp_s.append(v[n_p:].reshape(ndb, dseq, D_B))
            reps = CHUNK_B // dseq
            w_s = cmlp_w_s[j]
            w_blk = jax.vmap(lambda m: jnp.kron(jnp.eye(reps, dtype=F32), m))(w_s[:, :dseq, :dseq])
            wmix = jnp.stack([w_s, w_blk])
            b_t = cmlp_b_s[j].T
            bmix = jnp.stack([b_t, jnp.tile(b_t[:dseq], (reps, 1))])
            bmix = jnp.pad(bmix, ((0, 0), (0, 0), (0, LANES - H_B)))
            x = _post_cmlp_call(x1, u, v, n_p, n_s, wmix, bmix, cmlp_w_out[j].astype(BF16), nw, wi2, wo2,
                                pair_out=last)
        else:
            wm, ws = _split_in_proj(ssd_w_in[j], D_INNER_C + D_XBC_C)
            x1, pm, ps = _pre_call(x, n_p, n_s, nw, wi1, wo1, wm, ws)
            params = (ssd_conv_w[j], ssd_conv_b[j].reshape(1, -1), _pad_lanes(ssd_dt_bias[j]),
                      _pad_lanes(ssd_a_log[j]), jnp.repeat(ssd_d[j], P_C).reshape(1, -1),
                      ssd_norm_w[j].reshape(1, -1))
            kw = dict(d_out=D_INNER_C, o_dtype=BF16, d_scratch=D_INNER_C, nbat=1)
            o_p, cb_p, ssd_p = _seq_call(
                _ssd_kernel, pm, ps, jnp.zeros((1, nb, CONV_W - 1, D_XBC_C), F32),
                jnp.zeros((1, nb, H_C, P_C, N_C), F32), ssd_p, params, layer=0,
                n_seq=nb, seq_len=seq, c=min(CHUNK_C, seq), nseq=1, row_off=0, **kw)
            c_s = min(CHUNK_C, dseq)
            o_s, cb_s, ssd_s = _seq_call(
                _ssd_kernel, pm, ps, state_ssd_conv, state_ssd, ssd_s, params, layer=j,
                n_seq=ndb, seq_len=dseq, c=c_s, nseq=SSD_STACK_ROWS // c_s if dseq == c_s else 1, row_off=n_p,
                **kw)
            ssd_conv_p.append(cb_p); ssd_conv_s.append(cb_s)
            x = _post_call(x1, (o_p, o_s), n_p, n_s, ssd_w_out[j].astype(BF16), nw, wi2, wo2, pair_out=last)

    return (x[0].reshape(nb, seq, D_MODEL), x[1].reshape(ndb, dseq, D_MODEL),
            gdn_p, jnp.stack(gdn_conv_p), ssd_p, jnp.stack(ssd_conv_p),
            gdn_s, jnp.stack(gdn_conv_s), ssd_s, jnp.stack(ssd_conv_s),
            jnp.stack(cmlp_s))
```

```python
import functools
import math

import jax
import jax.numpy as jnp
from jax import lax
from jax.experimental import pallas as pl
from jax.experimental.pallas import tpu as pltpu

F32 = jnp.float32
BF16 = jnp.bfloat16
HI = lax.Precision.HIGHEST

D_MODEL = 1024
D_FF = 2816
EPS = 1e-6
CONV_W = 4
H_A, DK_A, DV_A = 8, 128, 128
D_QK_A, D_V_A = H_A * DK_A, H_A * DV_A
D_CONV_A = 2 * D_QK_A + D_V_A
CHUNK_A = 64
CHUNK_B, D_B, H_B = 128, 2 * D_MODEL, 8
DG_B = D_B // H_B
D_INNER_C, P_C, G_C, N_C = 2 * D_MODEL, 64, 4, 128
H_C = D_INNER_C // P_C
R_C = H_C // G_C
D_XBC_C = D_INNER_C + 2 * G_C * N_C
CHUNK_C = 128

LANES = 128
ROW_TILE = 256
VMEM_LIMIT_BYTES = 60 * 1024 * 1024
CONV_PAD = 8
PROJ_CHUNK = 1024
GDN_PROMPT_GROUPS = 4
SSD_STACK_ROWS = 64

_NT = (((1,), (1,)), ((), ()))
_TN = (((0,), (0,)), ((), ()))


def _dot(a, b):
    return jnp.dot(a, b, preferred_element_type=F32)


def _dot_hi(a, b):
    return jnp.dot(a, b, precision=HI, preferred_element_type=F32)


def _dot_nt(a, b):
    return lax.dot_general(a, b, _NT, preferred_element_type=F32)


def _dot_tn(a, b):
    return lax.dot_general(a, b, _TN, preferred_element_type=F32)


def _silu(x):
    return x * jax.nn.sigmoid(x)


def _softplus(x):
    return jnp.maximum(x, 0.0) + jnp.log1p(jnp.exp(-jnp.abs(x)))


def _rms(x, w):
    return x * lax.rsqrt(jnp.mean(x * x, axis=-1, keepdims=True) + EPS) * w


def _half_ffn(x, w_pre, w_post, wi_ref, wo_ref):
    h = _rms(x, w_pre).astype(BF16)
    gu = _dot(h, wi_ref[...])
    a = (_silu(gu[:, :D_FF]) * gu[:, D_FF:]).astype(BF16)
    return x + 0.5 * _rms(_dot(a, wo_ref[...]), w_post)


def _dot_into(out_ref, h, w_ref):
    width = w_ref.shape[1]
    for c0 in range(0, width, PROJ_CHUNK):
        c1 = min(c0 + PROJ_CHUNK, width)
        out_ref[:, c0:c1] = _dot(h, w_ref[:, c0:c1])


def _read_rows(refs, is_p, dtype=None):
    vals = [r[...] if dtype is None else r[...].astype(dtype) for r in refs]
    return vals[0] if len(vals) == 1 else jnp.where(is_p, vals[0], vals[1])


def _write_rows(refs, is_p, val):
    if len(refs) == 1:
        refs[0][...] = val
        return

    @pl.when(is_p)
    def _():
        refs[0][...] = val

    @pl.when(jnp.logical_not(is_p))
    def _():
        refs[1][...] = val


def _pre_kernel(*refs, npt, n_x):
    x_refs = refs[:n_x]
    nw_ref, wi_ref, wo_ref, wm_ref, ws_ref, x1_ref, pm_ref, ps_ref = refs[n_x:]
    is_p = pl.program_id(0) < npt
    nw = nw_ref[...]
    x1 = _half_ffn(_read_rows(x_refs, is_p), nw[0:1], nw[1:2], wi_ref, wo_ref)
    x1_ref[...] = x1
    h = _rms(x1, nw[2:3]).astype(BF16)
    _dot_into(pm_ref, h, wm_ref)
    ps_ref[...] = _dot(h, ws_ref[...])


def _pre_cmlp_kernel(*refs, npt, n_x):
    x_refs = refs[:n_x]
    nw_ref, wi_ref, wo_ref, wm_ref, b_ref, lnw_ref, lnb_ref, x1_ref, u_ref, v_ref = refs[n_x:]
    is_p = pl.program_id(0) < npt
    nw = nw_ref[...]
    x1 = _half_ffn(_read_rows(x_refs, is_p), nw[0:1], nw[1:2], wi_ref, wo_ref)
    x1_ref[...] = x1
    h = _rms(x1, nw[2:3]).astype(BF16)
    p = _dot(h, wm_ref[...]) + b_ref[...]
    ge = 0.5 * p * (1.0 + lax.erf(p * (1.0 / math.sqrt(2.0))))
    u_ref[...] = ge[:, :D_B]
    v = ge[:, D_B:]
    mu = jnp.mean(v, axis=-1, keepdims=True)
    vc = v - mu
    var = jnp.mean(vc * vc, axis=-1, keepdims=True)
    v_ref[...] = vc * lax.rsqrt(var + EPS) * lnw_ref[...] + lnb_ref[...]


def _post_kernel(x1_ref, op_ref, os_ref, wom_ref, nw_ref, wi_ref, wo_ref, *x3_refs, npt):
    is_p = pl.program_id(0) < npt
    nw = nw_ref[...]
    m = _dot(_read_rows((op_ref, os_ref), is_p, BF16), wom_ref[...])
    x2 = x1_ref[...] + _rms(m, nw[3:4])
    _write_rows(x3_refs, is_p, _half_ffn(x2, nw[4:5], nw[5:6], wi_ref, wo_ref))


def _post_cmlp_kernel(x1_ref, u_ref, v_ref, wmix_ref, bmix_ref, wom_ref, nw_ref, wi_ref, wo_ref, *x3_and_scratch,
                      npt):
    x3_refs, a_ref = x3_and_scratch[:-1], x3_and_scratch[-1]
    is_p = pl.program_id(0) < npt
    nw = nw_ref[...]
    row = lax.broadcasted_iota(jnp.int32, (CHUNK_B, CHUNK_B), 0)
    col = lax.broadcasted_iota(jnp.int32, (CHUNK_B, CHUNK_B), 1)
    bmix = bmix_ref[...]
    for h in range(H_B):
        w = jnp.where(row >= col, wmix_ref[h], 0.0).astype(BF16)
        cs = slice(h * DG_B, (h + 1) * DG_B)
        for r in range(ROW_TILE // CHUNK_B):
            rs = slice(r * CHUNK_B, (r + 1) * CHUNK_B)
            mixed = _dot(w, v_ref[rs, cs].astype(BF16)) + bmix[:, h:h + 1]
            a_ref[rs, cs] = (u_ref[rs, cs] * mixed).astype(BF16)
    m = _dot(a_ref[...], wom_ref[...])
    x2 = x1_ref[...] + _rms(m, nw[3:4])
    _write_rows(x3_refs, is_p, _half_ffn(x2, nw[4:5], nw[5:6], wi_ref, wo_ref))


def _resident(a, lead=()):
    rest = a.shape[len(lead):]
    return pl.BlockSpec((None,) * len(lead) + rest, lambda i: tuple(lead) + (0,) * len(rest),
                        pipeline_mode=pl.Buffered(1))


def _ffn_specs(ffn):
    w_in, w_out, lead = ffn
    return [_resident(w_in, lead), _resident(w_out, lead)]


def _rows(width):
    return pl.BlockSpec((ROW_TILE, width), lambda i: (i, 0))


def _pair_specs(width, npt):
    return [pl.BlockSpec((ROW_TILE, width), lambda i: (jnp.minimum(i, npt - 1), 0)),
            pl.BlockSpec((ROW_TILE, width), lambda i: (jnp.maximum(i - npt, 0), 0))]


def _x_specs(x, npt):
    return _pair_specs(D_MODEL, npt) if isinstance(x, tuple) else [_rows(D_MODEL)]


def _x_out(n_p, n_s, npt, pair):
    if pair:
        return _pair_specs(D_MODEL, npt), [jax.ShapeDtypeStruct((n_p, D_MODEL), F32),
                                           jax.ShapeDtypeStruct((n_s, D_MODEL), F32)]
    return [_rows(D_MODEL)], [jax.ShapeDtypeStruct((n_p + n_s, D_MODEL), F32)]


def _as_list(x):
    return list(x) if isinstance(x, tuple) else [x]


def _row_call(body, n_p, n_s, in_arrays, in_specs, out_specs, out_shape, scratch_shapes=()):
    assert n_p % ROW_TILE == 0 and n_s % ROW_TILE == 0
    return pl.pallas_call(
        body,
        grid=((n_p + n_s) // ROW_TILE,),
        in_specs=in_specs,
        out_specs=out_specs,
        out_shape=out_shape,
        scratch_shapes=list(scratch_shapes),
        compiler_params=pltpu.CompilerParams(dimension_semantics=("arbitrary",),
                                             vmem_limit_bytes=VMEM_LIMIT_BYTES),
    )(*in_arrays)


def _pre_call(x, n_p, n_s, nw, ffn, wm, ws):
    npt = n_p // ROW_TILE
    xs = _as_list(x)
    widths = (D_MODEL, wm.shape[1], ws.shape[1])
    return _row_call(
        functools.partial(_pre_kernel, npt=npt, n_x=len(xs)), n_p, n_s, (*xs, nw, *ffn[:2], wm, ws),
        _x_specs(x, npt) + [_resident(nw)] + _ffn_specs(ffn) + [_resident(wm), _resident(ws)],
        [_rows(w) for w in widths], [jax.ShapeDtypeStruct((n_p + n_s, w), F32) for w in widths])


def _pre_cmlp_call(x, n_p, n_s, nw, ffn, wm, b, lnw, lnb):
    npt = n_p // ROW_TILE
    xs = _as_list(x)
    widths = (D_MODEL, D_B, D_B)
    return _row_call(
        functools.partial(_pre_cmlp_kernel, npt=npt, n_x=len(xs)), n_p, n_s, (*xs, nw, *ffn[:2], wm, b, lnw, lnb),
        _x_specs(x, npt) + [_resident(nw)] + _ffn_specs(ffn) + [_resident(a) for a in (wm, b, lnw, lnb)],
        [_rows(w) for w in widths], [jax.ShapeDtypeStruct((n_p + n_s, w), F32) for w in widths])


def _post_call(x1, o, n_p, n_s, wom, nw, ffn, *, pair_out):
    npt = n_p // ROW_TILE
    out_specs, out_shape = _x_out(n_p, n_s, npt, pair_out)
    out = _row_call(
        functools.partial(_post_kernel, npt=npt), n_p, n_s, (x1, *o, wom, nw, *ffn[:2]),
        [_rows(D_MODEL)] + _pair_specs(o[0].shape[1], npt) + [_resident(wom), _resident(nw)] + _ffn_specs(ffn),
        out_specs, out_shape)
    return tuple(out) if pair_out else out[0]


def _post_cmlp_call(x1, u, v, n_p, n_s, wmix, bmix, wom, nw, ffn, *, pair_out):
    npt = n_p // ROW_TILE
    out_specs, out_shape = _x_out(n_p, n_s, npt, pair_out)
    out = _row_call(
        functools.partial(_post_cmlp_kernel, npt=npt), n_p, n_s, (x1, u, v, wmix, bmix, wom, nw, *ffn[:2]),
        [_rows(D_MODEL), _rows(D_B), _rows(D_B),
         pl.BlockSpec((None, H_B, CHUNK_B, CHUNK_B), lambda i: (i // npt, 0, 0, 0)),
         pl.BlockSpec((None, CHUNK_B, LANES), lambda i: (i // npt, 0, 0))]
        + [_resident(wom), _resident(nw)] + _ffn_specs(ffn),
        out_specs, out_shape, scratch_shapes=[pltpu.VMEM((ROW_TILE, D_B), BF16)])
    return tuple(out) if pair_out else out[0]


def _causal_conv(x, xp_ref, cw, c):
    xp_ref[CONV_PAD:CONV_PAD + c, :] = x
    y = cw[3:4] * x
    for j in range(CONV_W - 1):
        y = y + cw[j:j + 1] * xp_ref[CONV_PAD - 3 + j:CONV_PAD - 3 + j + c, :]
    tail = xp_ref[CONV_PAD + c - 3:CONV_PAD + c, :]
    xp_ref[CONV_PAD - 3:CONV_PAD, :] = tail
    return y, tail


def _seq_masks(rows, c):
    row = lax.broadcasted_iota(jnp.int32, (rows, rows), 0)
    col = lax.broadcasted_iota(jnp.int32, (rows, rows), 1)
    same = _same_block(row, col, c) if c < rows else (row >= 0)
    return row, col, same


def _same_block(row, col, k):
    shift = int(math.log2(k))
    return (row >> shift) == (col >> shift)


def _cumsum_rows(g, causal, same, c):
    rows = g.shape[0]
    tril = causal.astype(F32)
    big = _dot_hi(tril, g)
    big_t = lax.dot_general(g, tril, (((0,), (1,)), ((), ())), precision=HI, preferred_element_type=F32)
    tot = _dot_hi(same.astype(F32), g) if c < rows else big[rows - 1:rows, :]
    return big, big_t, tot


def _split(a):
    hi = a.astype(BF16)
    return hi, (a - hi.astype(F32)).astype(BF16)


def _unit_lower_inverses(ms, row, col, c):
    eye = (row == col).astype(F32)
    base = _same_block(row, col, 8)
    ns = [-jnp.where(base, m, 0.0) for m in ms]
    ts = [eye + n for n in ns]
    ps = [n.astype(BF16) for n in ns]
    ps = [_dot(p, p) for p in ps]
    for it in range(2):
        ps = [p.astype(BF16) for p in ps]
        ts = [t + _dot(t.astype(BF16), p) for t, p in zip(ts, ps)]
        if it == 0:
            ps = [_dot(p, p) for p in ps]
    k = 16
    while k <= c:
        off_mask = _same_block(row, col, k) & jnp.logical_not(_same_block(row, col, k // 2))
        offs = [jnp.where(off_mask, m, 0.0).astype(BF16) for m in ms]
        t16 = [t.astype(BF16) for t in ts]
        xs = [_dot(t, o).astype(BF16) for t, o in zip(t16, offs)]
        ts = [t - _dot(x, t2) for t, x, t2 in zip(ts, xs, t16)]
        k *= 2
    return ts


def _seq_refs(refs, nbat, n_prev, n_params):
    pm_refs, ps_refs = refs[:nbat], refs[nbat:2 * nbat]
    i = 2 * nbat
    cb_ref, s0_ref = refs[i:i + 2]
    i += 2
    prev_ref = refs[i] if n_prev else None
    i += 1 if n_prev else 0
    params = refs[i:i + n_params]
    og_ref, ncb_ref, sall_ref, xp_ref, y_ref = refs[i + n_params:]

    @pl.when(pl.program_id(1) == 0)
    def _():
        xp_ref[:, CONV_PAD - 3:CONV_PAD, :] = cb_ref[...]
        sall_ref[n_prev] = s0_ref[...]
        if n_prev:
            sall_ref[0:n_prev] = prev_ref[...]

    return pm_refs, ps_refs, params, og_ref, ncb_ref, sall_ref.at[n_prev], xp_ref, y_ref


def _gdn_kernel(*refs, c, nseq, nbat, n_prev):
    pm_refs, ps_refs, params, og_ref, ncb_ref, s_ref, xp_ref, y_ref = _seq_refs(refs, nbat, n_prev, 4)
    cw_ref, alog_ref, dtb_ref, nw_ref = params
    rows = c * nseq
    cw = cw_ref[...]
    nw = nw_ref[...]
    row, col, same = _seq_masks(rows, c)
    causal = same & (row >= col)
    strict = same & (row > col)

    ms, attn, rhs, qe, kd, tots = [], [], [], [], [], []
    for a in range(nbat):
        for s in range(nseq):
            rs = slice(s * c, (s + 1) * c)
            y_s, tail = _causal_conv(pm_refs[a][rs, :D_CONV_A], xp_ref.at[a * nseq + s], cw, c)
            ncb_ref[a * nseq + s] = tail
            y_ref[a, rs, :] = _silu(y_s)
        sm = ps_refs[a][...]
        g = -jnp.exp(alog_ref[...]) * _softplus(sm + dtb_ref[...])
        beta_all = jax.nn.sigmoid(sm)
        big, big_t, tot = _cumsum_rows(g, causal, same, c)
        tots.append(tot)
        for h in range(H_A):
            q = y_ref[a, :, h * DK_A:(h + 1) * DK_A]
            k = y_ref[a, :, D_QK_A + h * DK_A:D_QK_A + (h + 1) * DK_A]
            v = y_ref[a, :, 2 * D_QK_A + h * DV_A:2 * D_QK_A + (h + 1) * DV_A]
            q = q * lax.rsqrt(jnp.sum(q * q, axis=-1, keepdims=True) + EPS) * (DK_A ** -0.5)
            k = k * lax.rsqrt(jnp.sum(k * k, axis=-1, keepdims=True) + EPS)
            gc = big[:, h:h + 1]
            gr = big_t[h:h + 1, :]
            beta = beta_all[:, H_A + h:H_A + h + 1]
            gamma = jnp.exp(jnp.where(causal, gc - gr, -jnp.inf))
            eg = jnp.exp(gc)
            kb = k * beta
            k16 = k.astype(BF16)
            ms.append(jnp.where(strict, _dot_nt(kb.astype(BF16), k16) * gamma, 0.0))
            attn.append((_dot_nt(q.astype(BF16), k16) * gamma).astype(BF16))
            rhs.append(jnp.concatenate([v * beta, kb * eg], axis=1).astype(BF16))
            qe.append(q * eg)
            kd.append(k * jnp.exp(tot[:, h:h + 1] - gc))

    ts = _unit_lower_inverses(ms, row, col, c)
    sols = [_dot(t.astype(BF16), r) for t, r in zip(ts, rhs)]

    probs = [(a, h) for a in range(nbat) for h in range(H_A)]
    seqs = range(nseq)
    rs = [slice(s * c, (s + 1) * c) for s in seqs]
    s_old = [[s_ref[a * nseq + s, h] for s in seqs] for a, h in probs]
    s16 = [[x.astype(BF16) for x in row_] for row_ in s_old]
    w_s = [[_dot(sols[i][rs[s], DV_A:].astype(BF16), s16[i][s]) for s in seqs] for i in range(len(probs))]
    q_s = [[_dot(qe[i][rs[s]].astype(BF16), s16[i][s]) for s in seqs] for i in range(len(probs))]
    vn = [[sols[i][rs[s], :DV_A] - w_s[i][s] for s in seqs] for i in range(len(probs))]
    vn16 = [[x.astype(BF16) for x in row_] for row_ in vn]
    vn_all = [row_[0] if nseq == 1 else jnp.concatenate(row_, axis=0) for row_ in vn16]
    o = [(q_s[i][0] if nseq == 1 else jnp.concatenate(q_s[i], axis=0)) + _dot(attn[i], vn_all[i])
         for i in range(len(probs))]
    kv = [[_dot_tn(kd[i][rs[s]].astype(BF16), vn16[i][s]) for s in seqs] for i in range(len(probs))]
    for i, (a, h) in enumerate(probs):
        for s in seqs:
            decay = jnp.exp(tots[a][s * c:s * c + 1, h:h + 1] if nseq > 1 else tots[a][:, h:h + 1])
            s_ref[a * nseq + s, h] = s_old[i][s] * decay + kv[i][s]
        gate = pm_refs[a][:, D_CONV_A + h * DV_A:D_CONV_A + (h + 1) * DV_A]
        og_ref[a, :, h * DV_A:(h + 1) * DV_A] = (_rms(o[i], nw) * _silu(gate)).astype(og_ref.dtype)


def _ssd_kernel(*refs, c, nseq, nbat, n_prev):
    assert nbat == 1
    pm_refs, ps_refs, params, og_ref, ncb_ref, s_ref, xp_ref, y_ref = _seq_refs(refs, nbat, n_prev, 6)
    cw_ref, cbias_ref, dtb_ref, alog_ref, dskip_ref, nw_ref = params
    pm_ref, ps_ref, og_ref, y_ref = pm_refs[0], ps_refs[0], og_ref.at[0], y_ref.at[0]
    rows = c * nseq

    cw, cbias = cw_ref[...], cbias_ref[...]
    ys = []
    for s in range(nseq):
        y_s, tail = _causal_conv(pm_ref[s * c:(s + 1) * c, D_INNER_C:D_INNER_C + D_XBC_C], xp_ref.at[s], cw, c)
        ncb_ref[s] = tail
        ys.append(_silu(y_s + cbias))
    y = ys[0] if nseq == 1 else jnp.concatenate(ys, axis=0)
    x = y[:, :D_INNER_C]

    dt = _softplus(ps_ref[...] + dtb_ref[...])
    da = dt * (-jnp.exp(alog_ref[...]))
    row, col, same = _seq_masks(rows, c)
    causal = same & (row >= col)
    big, big_t, g_last = _cumsum_rows(da, causal, same, c)
    e_row = lax.broadcasted_iota(jnp.int32, (LANES, D_INNER_C), 0)
    e_col = lax.broadcasted_iota(jnp.int32, (LANES, D_INNER_C), 1)
    expand = (e_col // P_C == e_row).astype(BF16)

    def expand_heads(a):
        hi, lo = _split(a)
        return _dot(hi, expand) + _dot(lo, expand)

    dt_e = expand_heads(dt)
    eg_e = expand_heads(jnp.exp(big))
    dec_e = expand_heads(jnp.exp(g_last - big))
    xdt = x * dt_e
    xdt16 = xdt.astype(BF16)

    xdec = xdt * dec_e

    for g in range(G_C):
        bg = y[:, D_INNER_C + g * N_C:D_INNER_C + (g + 1) * N_C]
        cg = y[:, D_INNER_C + (G_C + g) * N_C:D_INNER_C + (G_C + g + 1) * N_C]
        gs = slice(g * R_C * P_C, (g + 1) * R_C * P_C)
        cbm = _dot_nt(cg.astype(BF16), bg.astype(BF16))
        y_off, st = [], []
        for s in range(nseq):
            rs = slice(s * c, (s + 1) * c)
            s_g = s_ref[s, g * R_C:(g + 1) * R_C].reshape(R_C * P_C, N_C)
            y_off.append(_dot_nt(cg[rs].astype(BF16), s_g.astype(BF16)))
            st.append(_dot_tn(xdec[rs, gs].astype(BF16), bg[rs].astype(BF16)))
        y_off = (y_off[0] if nseq == 1 else jnp.concatenate(y_off, axis=0)) * eg_e[:, gs]
        for r in range(R_C):
            h = g * R_C + r
            seg = jnp.exp(jnp.where(causal, big[:, h:h + 1] - big_t[h:h + 1, :], -jnp.inf))
            yd = _dot((cbm * seg).astype(BF16), xdt16[:, h * P_C:(h + 1) * P_C])
            y_ref[:, h * P_C:(h + 1) * P_C] = yd + y_off[:, r * P_C:(r + 1) * P_C]
            for s in range(nseq):
                decay = jnp.exp(g_last[s * c:s * c + 1, h:h + 1] if nseq > 1 else g_last[:, h:h + 1])
                s_ref[s, h] = s_ref[s, h] * decay + st[s][r * P_C:(r + 1) * P_C, :]

    zg = pm_ref[:, :D_INNER_C]
    yg = (y_ref[...] + x * dskip_ref[...]) * _silu(zg)
    nw = nw_ref[...]
    width = D_INNER_C // G_C
    for g in range(G_C):
        gs = slice(g * width, (g + 1) * width)
        og_ref[:, gs] = _rms(yg[:, gs], nw[:, gs]).astype(og_ref.dtype)


def _seq_call(body, pm, ps, conv_buf, s0, prev, params, *, layer, n_seq, seq_len, c, nseq, nbat, row_off, d_out,
              o_dtype, d_scratch):
    nz = seq_len // c
    rows = nseq * c
    per_step = nseq * nbat
    assert nseq == 1 or nz == 1
    assert n_seq % per_step == 0 and row_off % rows == 0
    blk_off = row_off // rows
    d_conv = conv_buf.shape[-1]
    state_shape = s0.shape[2:]
    n_prev = 0 if prev is None else prev.shape[0]
    zeros = (0,) * len(state_shape)

    def const(a):
        return pl.BlockSpec(a.shape, lambda b, z: (0,) * a.ndim)

    def row_blocks(width):
        return [pl.BlockSpec((rows, width), lambda b, z, a=a: (blk_off + (b * nbat + a) * nz + z, 0))
                for a in range(nbat)]

    prev_in = [] if prev is None else [prev]
    prev_specs = [] if prev is None else [
        pl.BlockSpec((n_prev, per_step) + state_shape, lambda b, z: (0, b) + zeros)]
    og, ncb, states = pl.pallas_call(
        functools.partial(body, c=c, nseq=nseq, nbat=nbat, n_prev=n_prev),
        grid=(n_seq // per_step, nz),
        in_specs=row_blocks(pm.shape[1]) + row_blocks(ps.shape[1])
        + [pl.BlockSpec((None, per_step, CONV_W - 1, d_conv), lambda b, z: (layer, b, 0, 0)),
           pl.BlockSpec((None, per_step) + state_shape, lambda b, z: (layer, b) + zeros)]
        + prev_specs + [const(a) for a in params],
        out_specs=[pl.BlockSpec((nbat, rows, d_out), lambda b, z: (b, z, 0)),
                   pl.BlockSpec((per_step, CONV_W - 1, d_conv), lambda b, z: (b, 0, 0)),
                   pl.BlockSpec((n_prev + 1, per_step) + state_shape, lambda b, z: (0, b) + zeros)],
        out_shape=[jax.ShapeDtypeStruct((n_seq // nseq, nz * rows, d_out), o_dtype),
                   jax.ShapeDtypeStruct((n_seq, CONV_W - 1, d_conv), F32),
                   jax.ShapeDtypeStruct((n_prev + 1, n_seq) + state_shape, F32)],
        scratch_shapes=[pltpu.VMEM((per_step, CONV_PAD + c, d_conv), F32),
                        pltpu.VMEM((nbat, rows, d_scratch), F32)],
        compiler_params=pltpu.CompilerParams(dimension_semantics=("arbitrary", "arbitrary"),
                                             vmem_limit_bytes=VMEM_LIMIT_BYTES),
    )(*([pm] * nbat), *([ps] * nbat), conv_buf, s0, *prev_in, *params)
    return og.reshape(n_seq * seq_len, d_out), ncb, states


def _pad_lanes(a):
    a = a.reshape(1, -1).astype(F32)
    return jnp.pad(a, ((0, 0), (0, LANES - a.shape[1])))


def _split_in_proj(w_in, n_main):
    wm = w_in[:, :n_main].astype(BF16)
    ws = jnp.pad(w_in[:, n_main:], ((0, 0), (0, LANES - (w_in.shape[1] - n_main)))).astype(BF16)
    return wm, ws


def kernel(x_prompt, x_sample, state_gdn, state_gdn_conv, state_ssd, state_ssd_conv, norm_w, ffn_w_in, ffn_w_out, gdn_w_in, gdn_conv_w, gdn_a_log, gdn_dt_bias, gdn_norm_w, gdn_w_out, cmlp_w_in, cmlp_b_in, cmlp_ln_w, cmlp_ln_b, cmlp_w_s, cmlp_b_s, cmlp_w_out, ssd_w_in, ssd_conv_w, ssd_conv_b, ssd_dt_bias, ssd_a_log, ssd_d, ssd_norm_w, ssd_w_out):
    nb, seq, _ = x_prompt.shape
    ndb, dseq, _ = x_sample.shape
    depth = norm_w.shape[0]
    n_p, n_s = nb * seq, ndb * dseq
    x = (x_prompt.reshape(n_p, D_MODEL), x_sample.reshape(n_s, D_MODEL))

    ffn_wi, ffn_wo = ffn_w_in.astype(BF16), ffn_w_out.astype(BF16)
    gdn_p = gdn_s = ssd_p = ssd_s = None
    gdn_conv_p, ssd_conv_p, gdn_conv_s, ssd_conv_s, cmlp_s = [], [], [], [], []
    for i in range(depth):
        kind, j, last = i % 3, i // 3, i == depth - 1
        nw = jnp.pad(norm_w[i], ((0, 2), (0, 0)))
        ffn1, ffn2 = (ffn_wi, ffn_wo, (i, 0)), (ffn_wi, ffn_wo, (i, 1))
        if kind == 0:
            wm, ws = _split_in_proj(gdn_w_in[j], D_CONV_A + D_V_A)
            x1, pm, ps = _pre_call(x, n_p, n_s, nw, ffn1, wm, ws)
            params = (gdn_conv_w[j], _pad_lanes(gdn_a_log[j]), _pad_lanes(gdn_dt_bias[j]),
                      gdn_norm_w[j].reshape(1, DV_A))
            kw = dict(d_out=D_V_A, o_dtype=BF16, d_scratch=D_CONV_A)
            o_p, cb_p, gdn_p = _seq_call(
                _gdn_kernel, pm, ps, jnp.zeros((1, nb, CONV_W - 1, D_CONV_A), F32),
                jnp.zeros((1, nb, H_A, DK_A, DV_A), F32), gdn_p, params, layer=0,
                n_seq=nb, seq_len=seq, c=min(CHUNK_A, seq), nseq=1, nbat=math.gcd(nb, GDN_PROMPT_GROUPS),
                row_off=0, **kw)
            c_s = min(CHUNK_A, dseq)
            o_s, cb_s, gdn_s = _seq_call(
                _gdn_kernel, pm, ps, state_gdn_conv, state_gdn, gdn_s, params, layer=j,
                n_seq=ndb, seq_len=dseq, c=c_s, nseq=CHUNK_A // c_s if dseq == c_s else 1, nbat=1, row_off=n_p,
                **kw)
            gdn_conv_p.append(cb_p); gdn_conv_s.append(cb_s)
            x = _post_call(x1, (o_p, o_s), n_p, n_s, gdn_w_out[j].astype(BF16), nw, ffn2, pair_out=last)
        elif kind == 1:
            x1, u, v = _pre_cmlp_call(x, n_p, n_s, nw, ffn1, cmlp_w_in[j].astype(BF16),
                                      cmlp_b_in[j].reshape(1, -1), cmlp_ln_w[j].reshape(1, -1),
                                      cmlp_ln_b[j].reshape(1, -1))
            cmlp_s.append(v[n_p:].reshape(ndb, dseq, D_B))
            reps = CHUNK_B // dseq
            w_s = cmlp_w_s[j]
            w_blk = jax.vmap(lambda m: jnp.kron(jnp.eye(reps, dtype=F32), m))(w_s[:, :dseq, :dseq])
            wmix = jnp.stack([w_s, w_blk])
            b_t = cmlp_b_s[j].T
            bmix = jnp.stack([b_t, jnp.tile(b_t[:dseq], (reps, 1))])
            bmix = jnp.pad(bmix, ((0, 0), (0, 0), (0, LANES - H_B)))
            x = _post_cmlp_call(x1, u, v, n_p, n_s, wmix, bmix, cmlp_w_out[j].astype(BF16), nw, ffn2,
                                pair_out=last)
        else:
            wm, ws = _split_in_proj(ssd_w_in[j], D_INNER_C + D_XBC_C)
            x1, pm, ps = _pre_call(x, n_p, n_s, nw, ffn1, wm, ws)
            params = (ssd_conv_w[j], ssd_conv_b[j].reshape(1, -1), _pad_lanes(ssd_dt_bias[j]),
                      _pad_lanes(ssd_a_log[j]), jnp.repeat(ssd_d[j], P_C).reshape(1, -1),
                      ssd_norm_w[j].reshape(1, -1))
            kw = dict(d_out=D_INNER_C, o_dtype=BF16, d_scratch=D_INNER_C, nbat=1)
            o_p, cb_p, ssd_p = _seq_call(
                _ssd_kernel, pm, ps, jnp.zeros((1, nb, CONV_W - 1, D_XBC_C), F32),
                jnp.zeros((1, nb, H_C, P_C, N_C), F32), ssd_p, params, layer=0,
                n_seq=nb, seq_len=seq, c=min(CHUNK_C, seq), nseq=1, row_off=0, **kw)
            c_s = min(CHUNK_C, dseq)
            o_s, cb_s, ssd_s = _seq_call(
                _ssd_kernel, pm, ps, state_ssd_conv, state_ssd, ssd_s, params, layer=j,
                n_seq=ndb, seq_len=dseq, c=c_s, nseq=SSD_STACK_ROWS // c_s if dseq == c_s else 1, row_off=n_p,
                **kw)
            ssd_conv_p.append(cb_p); ssd_conv_s.append(cb_s)
            x = _post_call(x1, (o_p, o_s), n_p, n_s, ssd_w_out[j].astype(BF16), nw, ffn2, pair_out=last)

    return (x[0].reshape(nb, seq, D_MODEL), x[1].reshape(ndb, dseq, D_MODEL),
            gdn_p, jnp.stack(gdn_conv_p), ssd_p, jnp.stack(ssd_conv_p),
            gdn_s, jnp.stack(gdn_conv_s), ssd_s, jnp.stack(ssd_conv_s),
            jnp.stack(cmlp_s))
```

```python
import functools
import math

import jax
import jax.numpy as jnp
from jax import lax
from jax.experimental import pallas as pl
from jax.experimental.pallas import tpu as pltpu

F32 = jnp.float32
BF16 = jnp.bfloat16
HI = lax.Precision.HIGHEST

D_MODEL = 1024
D_FF = 2816
EPS = 1e-6
CONV_W = 4
H_A, DK_A, DV_A = 8, 128, 128
D_QK_A, D_V_A = H_A * DK_A, H_A * DV_A
D_CONV_A = 2 * D_QK_A + D_V_A
CHUNK_A = 64
CHUNK_B, D_B, H_B = 128, 2 * D_MODEL, 8
DG_B = D_B // H_B
D_INNER_C, P_C, G_C, N_C = 2 * D_MODEL, 64, 4, 128
H_C = D_INNER_C // P_C
R_C = H_C // G_C
D_XBC_C = D_INNER_C + 2 * G_C * N_C
CHUNK_C = 128

LANES = 128
ROW_TILE = 256
POST_ROW_TILE = 512
VMEM_LIMIT_BYTES = 60 * 1024 * 1024
CONV_PAD = 8
PROJ_CHUNK = 1024
GDN_PROMPT_GROUPS = 4
SSD_STACK_ROWS = 64

_NT = (((1,), (1,)), ((), ()))
_TN = (((0,), (0,)), ((), ()))


def _dot(a, b):
    return jnp.dot(a, b, preferred_element_type=F32)


def _dot_hi(a, b):
    return jnp.dot(a, b, precision=HI, preferred_element_type=F32)


def _dot_nt(a, b):
    return lax.dot_general(a, b, _NT, preferred_element_type=F32)


def _dot_tn(a, b):
    return lax.dot_general(a, b, _TN, preferred_element_type=F32)


def _silu(x):
    return x * jax.nn.sigmoid(x)


def _softplus(x):
    return jnp.maximum(x, 0.0) + jnp.log1p(jnp.exp(-jnp.abs(x)))


def _rms(x, w):
    return x * lax.rsqrt(jnp.mean(x * x, axis=-1, keepdims=True) + EPS) * w


def _half_ffn(x, w_pre, w_post, wi_ref, wo_ref):
    h = _rms(x, w_pre).astype(BF16)
    gu = _dot(h, wi_ref[...])
    a = (_silu(gu[:, :D_FF]) * gu[:, D_FF:]).astype(BF16)
    return x + 0.5 * _rms(_dot(a, wo_ref[...]), w_post)


def _dot_into(out_ref, h, w_ref):
    width = w_ref.shape[1]
    for c0 in range(0, width, PROJ_CHUNK):
        c1 = min(c0 + PROJ_CHUNK, width)
        out_ref[:, c0:c1] = _dot(h, w_ref[:, c0:c1])


def _read_rows(refs, is_p, dtype=None):
    vals = [r[...] if dtype is None else r[...].astype(dtype) for r in refs]
    return vals[0] if len(vals) == 1 else jnp.where(is_p, vals[0], vals[1])


def _write_rows(refs, is_p, val):
    if len(refs) == 1:
        refs[0][...] = val
        return

    @pl.when(is_p)
    def _():
        refs[0][...] = val

    @pl.when(jnp.logical_not(is_p))
    def _():
        refs[1][...] = val


def _pre_kernel(*refs, npt, n_x):
    x_refs = refs[:n_x]
    nw_ref, wi_ref, wo_ref, wm_ref, ws_ref, x1_ref, pm_ref, ps_ref = refs[n_x:]
    is_p = pl.program_id(0) < npt
    nw = nw_ref[...]
    x1 = _half_ffn(_read_rows(x_refs, is_p), nw[0:1], nw[1:2], wi_ref, wo_ref)
    x1_ref[...] = x1
    h = _rms(x1, nw[2:3]).astype(BF16)
    _dot_into(pm_ref, h, wm_ref)
    ps_ref[...] = _dot(h, ws_ref[...])


def _pre_cmlp_kernel(*refs, npt, n_x):
    x_refs = refs[:n_x]
    nw_ref, wi_ref, wo_ref, wm_ref, b_ref, lnw_ref, lnb_ref, x1_ref, u_ref, v_ref = refs[n_x:]
    is_p = pl.program_id(0) < npt
    nw = nw_ref[...]
    x1 = _half_ffn(_read_rows(x_refs, is_p), nw[0:1], nw[1:2], wi_ref, wo_ref)
    x1_ref[...] = x1
    h = _rms(x1, nw[2:3]).astype(BF16)
    p = _dot(h, wm_ref[...]) + b_ref[...]
    ge = 0.5 * p * (1.0 + lax.erf(p * (1.0 / math.sqrt(2.0))))
    u_ref[...] = ge[:, :D_B]
    v = ge[:, D_B:]
    mu = jnp.mean(v, axis=-1, keepdims=True)
    vc = v - mu
    var = jnp.mean(vc * vc, axis=-1, keepdims=True)
    v_ref[...] = vc * lax.rsqrt(var + EPS) * lnw_ref[...] + lnb_ref[...]


def _post_kernel(x1_ref, op_ref, os_ref, wom_ref, nw_ref, wi_ref, wo_ref, *x3_refs, npt):
    is_p = pl.program_id(0) < npt
    nw = nw_ref[...]
    m = _dot(_read_rows((op_ref, os_ref), is_p, BF16), wom_ref[...])
    x2 = x1_ref[...] + _rms(m, nw[3:4])
    _write_rows(x3_refs, is_p, _half_ffn(x2, nw[4:5], nw[5:6], wi_ref, wo_ref))


def _post_cmlp_kernel(x1_ref, u_ref, v_ref, wmix_ref, bmix_ref, wom_ref, nw_ref, wi_ref, wo_ref, *x3_and_scratch,
                      npt):
    x3_refs, a_ref = x3_and_scratch[:-1], x3_and_scratch[-1]
    is_p = pl.program_id(0) < npt
    nw = nw_ref[...]
    row = lax.broadcasted_iota(jnp.int32, (CHUNK_B, CHUNK_B), 0)
    col = lax.broadcasted_iota(jnp.int32, (CHUNK_B, CHUNK_B), 1)
    bmix = bmix_ref[...]
    for h in range(H_B):
        w = jnp.where(row >= col, wmix_ref[h], 0.0).astype(BF16)
        cs = slice(h * DG_B, (h + 1) * DG_B)
        for r in range(ROW_TILE // CHUNK_B):
            rs = slice(r * CHUNK_B, (r + 1) * CHUNK_B)
            mixed = _dot(w, v_ref[rs, cs].astype(BF16)) + bmix[:, h:h + 1]
            a_ref[rs, cs] = (u_ref[rs, cs] * mixed).astype(BF16)
    m = _dot(a_ref[...], wom_ref[...])
    x2 = x1_ref[...] + _rms(m, nw[3:4])
    _write_rows(x3_refs, is_p, _half_ffn(x2, nw[4:5], nw[5:6], wi_ref, wo_ref))


def _resident(a, lead=()):
    rest = a.shape[len(lead):]
    return pl.BlockSpec((None,) * len(lead) + rest, lambda i: tuple(lead) + (0,) * len(rest),
                        pipeline_mode=pl.Buffered(1))


def _ffn_specs(ffn):
    w_in, w_out, lead = ffn
    return [_resident(w_in, lead), _resident(w_out, lead)]


def _rows(width, tile=ROW_TILE):
    return pl.BlockSpec((tile, width), lambda i: (i, 0))


def _pair_specs(width, npt, tile=ROW_TILE):
    return [pl.BlockSpec((tile, width), lambda i: (jnp.minimum(i, npt - 1), 0)),
            pl.BlockSpec((tile, width), lambda i: (jnp.maximum(i - npt, 0), 0))]


def _x_specs(x, npt):
    return _pair_specs(D_MODEL, npt) if isinstance(x, tuple) else [_rows(D_MODEL)]


def _x_out(n_p, n_s, npt, pair, tile=ROW_TILE):
    if pair:
        return _pair_specs(D_MODEL, npt, tile), [jax.ShapeDtypeStruct((n_p, D_MODEL), F32),
                                                 jax.ShapeDtypeStruct((n_s, D_MODEL), F32)]
    return [_rows(D_MODEL, tile)], [jax.ShapeDtypeStruct((n_p + n_s, D_MODEL), F32)]


def _as_list(x):
    return list(x) if isinstance(x, tuple) else [x]


def _row_call(body, n_p, n_s, in_arrays, in_specs, out_specs, out_shape, scratch_shapes=(), tile=ROW_TILE):
    assert n_p % tile == 0 and n_s % tile == 0
    return pl.pallas_call(
        body,
        grid=((n_p + n_s) // tile,),
        in_specs=in_specs,
        out_specs=out_specs,
        out_shape=out_shape,
        scratch_shapes=list(scratch_shapes),
        compiler_params=pltpu.CompilerParams(dimension_semantics=("arbitrary",),
                                             vmem_limit_bytes=VMEM_LIMIT_BYTES),
    )(*in_arrays)


def _pre_call(x, n_p, n_s, nw, ffn, wm, ws):
    npt = n_p // ROW_TILE
    xs = _as_list(x)
    widths = (D_MODEL, wm.shape[1], ws.shape[1])
    return _row_call(
        functools.partial(_pre_kernel, npt=npt, n_x=len(xs)), n_p, n_s, (*xs, nw, *ffn[:2], wm, ws),
        _x_specs(x, npt) + [_resident(nw)] + _ffn_specs(ffn) + [_resident(wm), _resident(ws)],
        [_rows(w) for w in widths], [jax.ShapeDtypeStruct((n_p + n_s, w), F32) for w in widths])


def _pre_cmlp_call(x, n_p, n_s, nw, ffn, wm, b, lnw, lnb):
    npt = n_p // ROW_TILE
    xs = _as_list(x)
    widths = (D_MODEL, D_B, D_B)
    return _row_call(
        functools.partial(_pre_cmlp_kernel, npt=npt, n_x=len(xs)), n_p, n_s, (*xs, nw, *ffn[:2], wm, b, lnw, lnb),
        _x_specs(x, npt) + [_resident(nw)] + _ffn_specs(ffn) + [_resident(a) for a in (wm, b, lnw, lnb)],
        [_rows(w) for w in widths], [jax.ShapeDtypeStruct((n_p + n_s, w), F32) for w in widths])


def _post_call(x1, o, n_p, n_s, wom, nw, ffn, *, pair_out):
    tile = POST_ROW_TILE
    npt = n_p // tile
    out_specs, out_shape = _x_out(n_p, n_s, npt, pair_out, tile)
    out = _row_call(
        functools.partial(_post_kernel, npt=npt), n_p, n_s, (x1, *o, wom, nw, *ffn[:2]),
        [_rows(D_MODEL, tile)] + _pair_specs(o[0].shape[1], npt, tile) + [_resident(wom), _resident(nw)]
        + _ffn_specs(ffn),
        out_specs, out_shape, tile=tile)
    return tuple(out) if pair_out else out[0]


def _post_cmlp_call(x1, u, v, n_p, n_s, wmix, bmix, wom, nw, ffn, *, pair_out):
    npt = n_p // ROW_TILE
    out_specs, out_shape = _x_out(n_p, n_s, npt, pair_out)
    out = _row_call(
        functools.partial(_post_cmlp_kernel, npt=npt), n_p, n_s, (x1, u, v, wmix, bmix, wom, nw, *ffn[:2]),
        [_rows(D_MODEL), _rows(D_B), _rows(D_B),
         pl.BlockSpec((None, H_B, CHUNK_B, CHUNK_B), lambda i: (i // npt, 0, 0, 0)),
         pl.BlockSpec((None, CHUNK_B, LANES), lambda i: (i // npt, 0, 0))]
        + [_resident(wom), _resident(nw)] + _ffn_specs(ffn),
        out_specs, out_shape, scratch_shapes=[pltpu.VMEM((ROW_TILE, D_B), BF16)])
    return tuple(out) if pair_out else out[0]


def _causal_conv(x, xp_ref, cw, c):
    xp_ref[CONV_PAD:CONV_PAD + c, :] = x
    y = cw[3:4] * x
    for j in range(CONV_W - 1):
        y = y + cw[j:j + 1] * xp_ref[CONV_PAD - 3 + j:CONV_PAD - 3 + j + c, :]
    tail = xp_ref[CONV_PAD + c - 3:CONV_PAD + c, :]
    xp_ref[CONV_PAD - 3:CONV_PAD, :] = tail
    return y, tail


def _seq_masks(rows, c):
    row = lax.broadcasted_iota(jnp.int32, (rows, rows), 0)
    col = lax.broadcasted_iota(jnp.int32, (rows, rows), 1)
    same = _same_block(row, col, c) if c < rows else (row >= 0)
    return row, col, same


def _same_block(row, col, k):
    shift = int(math.log2(k))
    return (row >> shift) == (col >> shift)


def _cumsum_rows(g, causal, same, c):
    rows = g.shape[0]
    tril = causal.astype(F32)
    big = _dot_hi(tril, g)
    big_t = lax.dot_general(g, tril, (((0,), (1,)), ((), ())), precision=HI, preferred_element_type=F32)
    tot = _dot_hi(same.astype(F32), g) if c < rows else big[rows - 1:rows, :]
    return big, big_t, tot


def _split(a):
    hi = a.astype(BF16)
    return hi, (a - hi.astype(F32)).astype(BF16)


def _unit_lower_inverses(ms, row, col, c):
    eye = (row == col).astype(F32)
    base = _same_block(row, col, 8)
    ns = [-jnp.where(base, m, 0.0) for m in ms]
    ts = [eye + n for n in ns]
    ps = [n.astype(BF16) for n in ns]
    ps = [_dot(p, p) for p in ps]
    for it in range(2):
        ps = [p.astype(BF16) for p in ps]
        ts = [t + _dot(t.astype(BF16), p) for t, p in zip(ts, ps)]
        if it == 0:
            ps = [_dot(p, p) for p in ps]
    k = 16
    while k <= c:
        off_mask = _same_block(row, col, k) & jnp.logical_not(_same_block(row, col, k // 2))
        offs = [jnp.where(off_mask, m, 0.0).astype(BF16) for m in ms]
        t16 = [t.astype(BF16) for t in ts]
        xs = [_dot(t, o).astype(BF16) for t, o in zip(t16, offs)]
        ts = [t - _dot(x, t2) for t, x, t2 in zip(ts, xs, t16)]
        k *= 2
    return ts


def _seq_refs(refs, nbat, n_prev, n_params):
    pm_refs, ps_refs = refs[:nbat], refs[nbat:2 * nbat]
    i = 2 * nbat
    cb_ref, s0_ref = refs[i:i + 2]
    i += 2
    prev_ref = refs[i] if n_prev else None
    i += 1 if n_prev else 0
    params = refs[i:i + n_params]
    og_ref, ncb_ref, sall_ref, xp_ref, y_ref = refs[i + n_params:]

    @pl.when(pl.program_id(1) == 0)
    def _():
        xp_ref[:, CONV_PAD - 3:CONV_PAD, :] = cb_ref[...]
        sall_ref[n_prev] = s0_ref[...]
        if n_prev:
            sall_ref[0:n_prev] = prev_ref[...]

    return pm_refs, ps_refs, params, og_ref, ncb_ref, sall_ref.at[n_prev], xp_ref, y_ref


def _gdn_kernel(*refs, c, nseq, nbat, n_prev):
    pm_refs, ps_refs, params, og_ref, ncb_ref, s_ref, xp_ref, y_ref = _seq_refs(refs, nbat, n_prev, 4)
    cw_ref, alog_ref, dtb_ref, nw_ref = params
    rows = c * nseq
    cw = cw_ref[...]
    nw = nw_ref[...]
    row, col, same = _seq_masks(rows, c)
    causal = same & (row >= col)
    strict = same & (row > col)

    ms, attn, rhs, qe, kd, tots = [], [], [], [], [], []
    for a in range(nbat):
        for s in range(nseq):
            rs = slice(s * c, (s + 1) * c)
            y_s, tail = _causal_conv(pm_refs[a][rs, :D_CONV_A], xp_ref.at[a * nseq + s], cw, c)
            ncb_ref[a * nseq + s] = tail
            y_ref[a, rs, :] = _silu(y_s)
        sm = ps_refs[a][...]
        g = -jnp.exp(alog_ref[...]) * _softplus(sm + dtb_ref[...])
        beta_all = jax.nn.sigmoid(sm)
        big, big_t, tot = _cumsum_rows(g, causal, same, c)
        tots.append(tot)
        for h in range(H_A):
            q = y_ref[a, :, h * DK_A:(h + 1) * DK_A]
            k = y_ref[a, :, D_QK_A + h * DK_A:D_QK_A + (h + 1) * DK_A]
            v = y_ref[a, :, 2 * D_QK_A + h * DV_A:2 * D_QK_A + (h + 1) * DV_A]
            q = q * lax.rsqrt(jnp.sum(q * q, axis=-1, keepdims=True) + EPS) * (DK_A ** -0.5)
            k = k * lax.rsqrt(jnp.sum(k * k, axis=-1, keepdims=True) + EPS)
            gc = big[:, h:h + 1]
            gr = big_t[h:h + 1, :]
            beta = beta_all[:, H_A + h:H_A + h + 1]
            gamma = jnp.exp(jnp.where(causal, gc - gr, -jnp.inf))
            eg = jnp.exp(gc)
            kb = k * beta
            k16 = k.astype(BF16)
            ms.append(jnp.where(strict, _dot_nt(kb.astype(BF16), k16) * gamma, 0.0))
            attn.append((_dot_nt(q.astype(BF16), k16) * gamma).astype(BF16))
            rhs.append(jnp.concatenate([v * beta, kb * eg], axis=1).astype(BF16))
            qe.append(q * eg)
            kd.append(k * jnp.exp(tot[:, h:h + 1] - gc))

    ts = _unit_lower_inverses(ms, row, col, c)
    sols = [_dot(t.astype(BF16), r) for t, r in zip(ts, rhs)]

    probs = [(a, h) for a in range(nbat) for h in range(H_A)]
    seqs = range(nseq)
    rs = [slice(s * c, (s + 1) * c) for s in seqs]
    s_old = [[s_ref[a * nseq + s, h] for s in seqs] for a, h in probs]
    s16 = [[x.astype(BF16) for x in row_] for row_ in s_old]
    w_s = [[_dot(sols[i][rs[s], DV_A:].astype(BF16), s16[i][s]) for s in seqs] for i in range(len(probs))]
    q_s = [[_dot(qe[i][rs[s]].astype(BF16), s16[i][s]) for s in seqs] for i in range(len(probs))]
    vn = [[sols[i][rs[s], :DV_A] - w_s[i][s] for s in seqs] for i in range(len(probs))]
    vn16 = [[x.astype(BF16) for x in row_] for row_ in vn]
    vn_all = [row_[0] if nseq == 1 else jnp.concatenate(row_, axis=0) for row_ in vn16]
    o = [(q_s[i][0] if nseq == 1 else jnp.concatenate(q_s[i], axis=0)) + _dot(attn[i], vn_all[i])
         for i in range(len(probs))]
    kv = [[_dot_tn(kd[i][rs[s]].astype(BF16), vn16[i][s]) for s in seqs] for i in range(len(probs))]
    for i, (a, h) in enumerate(probs):
        for s in seqs:
            decay = jnp.exp(tots[a][s * c:s * c + 1, h:h + 1] if nseq > 1 else tots[a][:, h:h + 1])
            s_ref[a * nseq + s, h] = s_old[i][s] * decay + kv[i][s]
        gate = pm_refs[a][:, D_CONV_A + h * DV_A:D_CONV_A + (h + 1) * DV_A]
        og_ref[a, :, h * DV_A:(h + 1) * DV_A] = (_rms(o[i], nw) * _silu(gate)).astype(og_ref.dtype)


def _ssd_kernel(*refs, c, nseq, nbat, n_prev):
    assert nbat == 1
    pm_refs, ps_refs, params, og_ref, ncb_ref, s_ref, xp_ref, y_ref = _seq_refs(refs, nbat, n_prev, 6)
    cw_ref, cbias_ref, dtb_ref, alog_ref, dskip_ref, nw_ref = params
    pm_ref, ps_ref, og_ref, y_ref = pm_refs[0], ps_refs[0], og_ref.at[0], y_ref.at[0]
    rows = c * nseq

    cw, cbias = cw_ref[...], cbias_ref[...]
    ys = []
    for s in range(nseq):
        y_s, tail = _causal_conv(pm_ref[s * c:(s + 1) * c, D_INNER_C:D_INNER_C + D_XBC_C], xp_ref.at[s], cw, c)
        ncb_ref[s] = tail
        ys.append(_silu(y_s + cbias))
    y = ys[0] if nseq == 1 else jnp.concatenate(ys, axis=0)
    x = y[:, :D_INNER_C]

    dt = _softplus(ps_ref[...] + dtb_ref[...])
    da = dt * (-jnp.exp(alog_ref[...]))
    row, col, same = _seq_masks(rows, c)
    causal = same & (row >= col)
    big, big_t, g_last = _cumsum_rows(da, causal, same, c)
    e_row = lax.broadcasted_iota(jnp.int32, (LANES, D_INNER_C), 0)
    e_col = lax.broadcasted_iota(jnp.int32, (LANES, D_INNER_C), 1)
    expand = (e_col // P_C == e_row).astype(BF16)

    def expand_heads(a):
        hi, lo = _split(a)
        return _dot(hi, expand) + _dot(lo, expand)

    dt_e = expand_heads(dt)
    eg_e = expand_heads(jnp.exp(big))
    dec_e = expand_heads(jnp.exp(g_last - big))
    xdt = x * dt_e
    xdt16 = xdt.astype(BF16)

    xdec = xdt * dec_e

    for g in range(G_C):
        bg = y[:, D_INNER_C + g * N_C:D_INNER_C + (g + 1) * N_C]
        cg = y[:, D_INNER_C + (G_C + g) * N_C:D_INNER_C + (G_C + g + 1) * N_C]
        gs = slice(g * R_C * P_C, (g + 1) * R_C * P_C)
        cbm = _dot_nt(cg.astype(BF16), bg.astype(BF16))
        y_off, st = [], []
        for s in range(nseq):
            rs = slice(s * c, (s + 1) * c)
            s_g = s_ref[s, g * R_C:(g + 1) * R_C].reshape(R_C * P_C, N_C)
            y_off.append(_dot_nt(cg[rs].astype(BF16), s_g.astype(BF16)))
            st.append(_dot_tn(xdec[rs, gs].astype(BF16), bg[rs].astype(BF16)))
        y_off = (y_off[0] if nseq == 1 else jnp.concatenate(y_off, axis=0)) * eg_e[:, gs]
        for r in range(R_C):
            h = g * R_C + r
            seg = jnp.exp(jnp.where(causal, big[:, h:h + 1] - big_t[h:h + 1, :], -jnp.inf))
            yd = _dot((cbm * seg).astype(BF16), xdt16[:, h * P_C:(h + 1) * P_C])
            y_ref[:, h * P_C:(h + 1) * P_C] = yd + y_off[:, r * P_C:(r + 1) * P_C]
            for s in range(nseq):
                decay = jnp.exp(g_last[s * c:s * c + 1, h:h + 1] if nseq > 1 else g_last[:, h:h + 1])
                s_ref[s, h] = s_ref[s, h] * decay + st[s][r * P_C:(r + 1) * P_C, :]

    zg = pm_ref[:, :D_INNER_C]
    yg = (y_ref[...] + x * dskip_ref[...]) * _silu(zg)
    nw = nw_ref[...]
    width = D_INNER_C // G_C
    for g in range(G_C):
        gs = slice(g * width, (g + 1) * width)
        og_ref[:, gs] = _rms(yg[:, gs], nw[:, gs]).astype(og_ref.dtype)


def _seq_call(body, pm, ps, conv_buf, s0, prev, params, *, layer, n_seq, seq_len, c, nseq, nbat, row_off, d_out,
              o_dtype, d_scratch):
    nz = seq_len // c
    rows = nseq * c
    per_step = nseq * nbat
    assert nseq == 1 or nz == 1
    assert n_seq % per_step == 0 and row_off % rows == 0
    blk_off = row_off // rows
    d_conv = conv_buf.shape[-1]
    state_shape = s0.shape[2:]
    n_prev = 0 if prev is None else prev.shape[0]
    zeros = (0,) * len(state_shape)

    def const(a):
        return pl.BlockSpec(a.shape, lambda b, z: (0,) * a.ndim)

    def row_blocks(width):
        return [pl.BlockSpec((rows, width), lambda b, z, a=a: (blk_off + (b * nbat + a) * nz + z, 0))
                for a in range(nbat)]

    prev_in = [] if prev is None else [prev]
    prev_specs = [] if prev is None else [
        pl.BlockSpec((n_prev, per_step) + state_shape, lambda b, z: (0, b) + zeros)]
    og, ncb, states = pl.pallas_call(
        functools.partial(body, c=c, nseq=nseq, nbat=nbat, n_prev=n_prev),
        grid=(n_seq // per_step, nz),
        in_specs=row_blocks(pm.shape[1]) + row_blocks(ps.shape[1])
        + [pl.BlockSpec((None, per_step, CONV_W - 1, d_conv), lambda b, z: (layer, b, 0, 0)),
           pl.BlockSpec((None, per_step) + state_shape, lambda b, z: (layer, b) + zeros)]
        + prev_specs + [const(a) for a in params],
        out_specs=[pl.BlockSpec((nbat, rows, d_out), lambda b, z: (b, z, 0)),
                   pl.BlockSpec((per_step, CONV_W - 1, d_conv), lambda b, z: (b, 0, 0)),
                   pl.BlockSpec((n_prev + 1, per_step) + state_shape, lambda b, z: (0, b) + zeros)],
        out_shape=[jax.ShapeDtypeStruct((n_seq // nseq, nz * rows, d_out), o_dtype),
                   jax.ShapeDtypeStruct((n_seq, CONV_W - 1, d_conv), F32),
                   jax.ShapeDtypeStruct((n_prev + 1, n_seq) + state_shape, F32)],
        scratch_shapes=[pltpu.VMEM((per_step, CONV_PAD + c, d_conv), F32),
                        pltpu.VMEM((nbat, rows, d_scratch), F32)],
        compiler_params=pltpu.CompilerParams(dimension_semantics=("arbitrary", "arbitrary"),
                                             vmem_limit_bytes=VMEM_LIMIT_BYTES),
    )(*([pm] * nbat), *([ps] * nbat), conv_buf, s0, *prev_in, *params)
    return og.reshape(n_seq * seq_len, d_out), ncb, states


def _pad_lanes(a):
    a = a.reshape(1, -1).astype(F32)
    return jnp.pad(a, ((0, 0), (0, LANES - a.shape[1])))


def _split_in_proj(w_in, n_main):
    wm = w_in[:, :n_main].astype(BF16)
    ws = jnp.pad(w_in[:, n_main:], ((0, 0), (0, LANES - (w_in.shape[1] - n_main)))).astype(BF16)
    return wm, ws


def kernel(x_prompt, x_sample, state_gdn, state_gdn_conv, state_ssd, state_ssd_conv, norm_w, ffn_w_in, ffn_w_out, gdn_w_in, gdn_conv_w, gdn_a_log, gdn_dt_bias, gdn_norm_w, gdn_w_out, cmlp_w_in, cmlp_b_in, cmlp_ln_w, cmlp_ln_b, cmlp_w_s, cmlp_b_s, cmlp_w_out, ssd_w_in, ssd_conv_w, ssd_conv_b, ssd_dt_bias, ssd_a_log, ssd_d, ssd_norm_w, ssd_w_out):
    nb, seq, _ = x_prompt.shape
    ndb, dseq, _ = x_sample.shape
    depth = norm_w.shape[0]
    n_p, n_s = nb * seq, ndb * dseq
    x = (x_prompt.reshape(n_p, D_MODEL), x_sample.reshape(n_s, D_MODEL))

    ffn_wi, ffn_wo = ffn_w_in.astype(BF16), ffn_w_out.astype(BF16)
    gdn_p = gdn_s = ssd_p = ssd_s = None
    gdn_conv_p, ssd_conv_p, gdn_conv_s, ssd_conv_s, cmlp_s = [], [], [], [], []
    for i in range(depth):
        kind, j, last = i % 3, i // 3, i == depth - 1
        nw = jnp.pad(norm_w[i], ((0, 2), (0, 0)))
        ffn1, ffn2 = (ffn_wi, ffn_wo, (i, 0)), (ffn_wi, ffn_wo, (i, 1))
        if kind == 0:
            wm, ws = _split_in_proj(gdn_w_in[j], D_CONV_A + D_V_A)
            x1, pm, ps = _pre_call(x, n_p, n_s, nw, ffn1, wm, ws)
            params = (gdn_conv_w[j], _pad_lanes(gdn_a_log[j]), _pad_lanes(gdn_dt_bias[j]),
                      gdn_norm_w[j].reshape(1, DV_A))
            kw = dict(d_out=D_V_A, o_dtype=BF16, d_scratch=D_CONV_A)
            o_p, cb_p, gdn_p = _seq_call(
                _gdn_kernel, pm, ps, jnp.zeros((1, nb, CONV_W - 1, D_CONV_A), F32),
                jnp.zeros((1, nb, H_A, DK_A, DV_A), F32), gdn_p, params, layer=0,
                n_seq=nb, seq_len=seq, c=min(CHUNK_A, seq), nseq=1, nbat=math.gcd(nb, GDN_PROMPT_GROUPS),
                row_off=0, **kw)
            c_s = min(CHUNK_A, dseq)
            o_s, cb_s, gdn_s = _seq_call(
                _gdn_kernel, pm, ps, state_gdn_conv, state_gdn, gdn_s, params, layer=j,
                n_seq=ndb, seq_len=dseq, c=c_s, nseq=CHUNK_A // c_s if dseq == c_s else 1, nbat=1, row_off=n_p,
                **kw)
            gdn_conv_p.append(cb_p); gdn_conv_s.append(cb_s)
            x = _post_call(x1, (o_p, o_s), n_p, n_s, gdn_w_out[j].astype(BF16), nw, ffn2, pair_out=last)
        elif kind == 1:
            x1, u, v = _pre_cmlp_call(x, n_p, n_s, nw, ffn1, cmlp_w_in[j].astype(BF16),
                                      cmlp_b_in[j].reshape(1, -1), cmlp_ln_w[j].reshape(1, -1),
                                      cmlp_ln_b[j].reshape(1, -1))
            cmlp_s.append(v[n_p:].reshape(ndb, dseq, D_B))
            reps = CHUNK_B // dseq
            w_s = cmlp_w_s[j]
            w_blk = jax.vmap(lambda m: jnp.kron(jnp.eye(reps, dtype=F32), m))(w_s[:, :dseq, :dseq])
            wmix = jnp.stack([w_s, w_blk])
            b_t = cmlp_b_s[j].T
            bmix = jnp.stack([b_t, jnp.tile(b_t[:dseq], (reps, 1))])
            bmix = jnp.pad(bmix, ((0, 0), (0, 0), (0, LANES - H_B)))
            x = _post_cmlp_call(x1, u, v, n_p, n_s, wmix, bmix, cmlp_w_out[j].astype(BF16), nw, ffn2,
                                pair_out=last)
        else:
            wm, ws = _split_in_proj(ssd_w_in[j], D_INNER_C + D_XBC_C)
            x1, pm, ps = _pre_call(x, n_p, n_s, nw, ffn1, wm, ws)
            params = (ssd_conv_w[j], ssd_conv_b[j].reshape(1, -1), _pad_lanes(ssd_dt_bias[j]),
                      _pad_lanes(ssd_a_log[j]), jnp.repeat(ssd_d[j], P_C).reshape(1, -1),
                      ssd_norm_w[j].reshape(1, -1))
            kw = dict(d_out=D_INNER_C, o_dtype=BF16, d_scratch=D_INNER_C, nbat=1)
            o_p, cb_p, ssd_p = _seq_call(
                _ssd_kernel, pm, ps, jnp.zeros((1, nb, CONV_W - 1, D_XBC_C), F32),
                jnp.zeros((1, nb, H_C, P_C, N_C), F32), ssd_p, params, layer=0,
                n_seq=nb, seq_len=seq, c=min(CHUNK_C, seq), nseq=1, row_off=0, **kw)
            c_s = min(CHUNK_C, dseq)
            o_s, cb_s, ssd_s = _seq_call(
                _ssd_kernel, pm, ps, state_ssd_conv, state_ssd, ssd_s, params, layer=j,
                n_seq=ndb, seq_len=dseq, c=c_s, nseq=SSD_STACK_ROWS // c_s if dseq == c_s else 1, row_off=n_p,
                **kw)
            ssd_conv_p.append(cb_p); ssd_conv_s.append(cb_s)
            x = _post_call(x1, (o_p, o_s), n_p, n_s, ssd_w_out[j].astype(BF16), nw, ffn2, pair_out=last)

    return (x[0].reshape(nb, seq, D_MODEL), x[1].reshape(ndb, dseq, D_MODEL),
            gdn_p, jnp.stack(gdn_conv_p), ssd_p, jnp.stack(ssd_conv_p),
            gdn_s, jnp.stack(gdn_conv_s), ssd_s, jnp.stack(ssd_conv_s),
            jnp.stack(cmlp_s))
```
